```python
import math
import jax, jax.numpy as jnp
from jax import lax
import numpy as np

D_MODEL = 1024
BATCH = 16
SEQ = 4096
DEPTH = 1

CHUNK = 64
Q_BLOCK = 128
TOKEN_BLOCK = 128
EPS = 1e-6

SB_HEADS = 8
SB_HEAD_DIM = 64
SB_WIDTH = SB_HEADS * SB_HEAD_DIM

MLA_HEADS = 8
MLA_NOPE_DIM = 64
MLA_ROPE_DIM = 32
MLA_QK_DIM = MLA_NOPE_DIM + MLA_ROPE_DIM
MLA_V_DIM = 64
MLA_Q_RANK = 384
MLA_KV_RANK = 256
MLA_WIDTH = MLA_HEADS * MLA_V_DIM
ROPE_THETA = 10000.0

MIX_WIDTH = SB_WIDTH + MLA_WIDTH
IN_SPLITS = (SB_WIDTH, 2 * SB_WIDTH, 3 * SB_WIDTH,
             3 * SB_WIDTH + MLA_Q_RANK,
             3 * SB_WIDTH + MLA_Q_RANK + MLA_KV_RANK)
IN_PROJ_WIDTH = 3 * SB_WIDTH + MLA_Q_RANK + MLA_KV_RANK + MLA_ROPE_DIM

PEER_HEADS = 8
PEER_N_KEYS = 128
PEER_N_EXPERTS = PEER_N_KEYS * PEER_N_KEYS
PEER_KEY_DIM = 256
PEER_HALF_DIM = PEER_KEY_DIM // 2
PEER_TOPK = 16

kernel_name = "hybrid_stickbreak_mla_peer_block"


def rms_norm(x, g):
    xf = x.astype(jnp.float32)
    y = xf * lax.rsqrt(jnp.mean(xf * xf, axis=-1, keepdims=True) + EPS)
    return (y * g.astype(jnp.float32)).astype(x.dtype)


def rope(x, pos):
    r = x.shape[-1]
    half = r // 2
    inv_freq = 1.0 / (ROPE_THETA ** (jnp.arange(half, dtype=jnp.float32) * (2.0 / r)))
    ang = pos.astype(jnp.float32)[..., None] * inv_freq
    cos = jnp.cos(ang)[:, :, None, :]
    sin = jnp.sin(ang)[:, :, None, :]
    xf = x.astype(jnp.float32)
    x1, x2 = xf[..., :half], xf[..., half:]
    out = jnp.concatenate([x1 * cos - x2 * sin, x1 * sin + x2 * cos], axis=-1)
    return out.astype(x.dtype)


def to_query_blocks(q):
    b, s, h, d = q.shape
    return q.reshape(b, s // Q_BLOCK, Q_BLOCK, h, d).transpose(1, 0, 3, 2, 4)


def from_query_blocks(o):
    nb, b, qb, h, d = o.shape
    return o.transpose(1, 0, 2, 3, 4).reshape(b, nb * qb, h * d)


def stick_breaking_attention(q, k, v):
    b, s, h, d = q.shape
    nb = s // Q_BLOCK
    scale = d ** -0.5
    kpos = jnp.arange(s)

    def block(args):
        qi, bi = args
        z = jnp.einsum('bhqd,bshd->bhqs', qi, k,
                       preferred_element_type=jnp.float32) * scale
        qpos = bi * Q_BLOCK + jnp.arange(Q_BLOCK)
        strict = kpos[None, :] < qpos[:, None]
        log_fail = jnp.where(strict, jax.nn.log_sigmoid(-z), 0.0)
        later = lax.cumsum(log_fail, axis=3, reverse=True) - log_fail
        a = jnp.where(strict, jnp.exp(jax.nn.log_sigmoid(z) + later), 0.0)
        return jnp.einsum('bhqs,bshd->bqhd', a.astype(v.dtype), v)

    out = lax.map(block, (to_query_blocks(q), jnp.arange(nb)))
    return from_query_blocks(out)


def chunk_causal_softmax_attention(q, k, v):
    b, s, h, dq = q.shape
    nb = s // Q_BLOCK
    scale = dq ** -0.5
    kchunk = jnp.arange(s) // CHUNK

    def block(args):
        qi, bi = args
        sc = jnp.einsum('bhqd,bshd->bhqs', qi, k,
                        preferred_element_type=jnp.float32) * scale
        qchunk = (bi * Q_BLOCK + jnp.arange(Q_BLOCK)) // CHUNK
        allowed = kchunk[None, :] <= qchunk[:, None]
        p = jax.nn.softmax(jnp.where(allowed, sc, -jnp.inf), axis=-1)
        return jnp.einsum('bhqs,bshd->bqhd', p.astype(v.dtype), v)

    out = lax.map(block, (to_query_blocks(q), jnp.arange(nb)))
    return from_query_blocks(out)


def peer_ffn(h, w_q, sub_keys, u, v):
    b, s, d = h.shape
    nb = (b * s) // TOKEN_BLOCK
    k = PEER_TOPK

    def block(xt):
        q = (xt @ w_q).reshape(TOKEN_BLOCK, PEER_HEADS, 2, PEER_HALF_DIM)
        sc = jnp.einsum('thcd,chnd->thcn', q, sub_keys,
                        preferred_element_type=jnp.float32)
        top_s, top_i = lax.top_k(sc, k)
        cand_s = top_s[:, :, 0, :, None] + top_s[:, :, 1, None, :]
        cand_i = top_i[:, :, 0, :, None] * PEER_N_KEYS + top_i[:, :, 1, None, :]
        best_s, best_pos = lax.top_k(cand_s.reshape(TOKEN_BLOCK, PEER_HEADS, k * k), k)
        idx = jnp.take_along_axis(cand_i.reshape(TOKEN_BLOCK, PEER_HEADS, k * k),
                                  best_pos, axis=-1)
        g = jax.nn.softmax(best_s, axis=-1)
        pre = jnp.einsum('td,thkd->thk', xt, u[idx], preferred_element_type=jnp.float32)
        coef = (g * jax.nn.gelu(pre, approximate=False)).astype(v.dtype)
        return jnp.einsum('thk,thkd->td', coef, v[idx])

    out = lax.map(block, h.reshape(nb, TOKEN_BLOCK, d))
    return out.reshape(b, s, d)


def setup_inputs(seed: int = 0) -> dict:
    key = jax.random.key(seed)
    ks = jax.random.split(key, 20)
    f32 = jnp.float32

    def w(k_, shape, fan_in):
        return jax.random.normal(k_, shape, f32) * (fan_in ** -0.5)

    def gain(k_, shape):
        return 1.0 + 0.02 * jax.random.normal(k_, shape, f32)

    x = jax.random.normal(ks[0], (BATCH, SEQ, D_MODEL), f32)
    offset = jax.random.randint(ks[1], (BATCH, 1), 0, 1024, dtype=jnp.int32)
    positions = (offset + jnp.arange(SEQ, dtype=jnp.int32)[None, :]).astype(jnp.int32)
    return {
        "x": x,
        "positions": positions,
        "attn_norm": gain(ks[2], (DEPTH, D_MODEL)),
        "w_in": w(ks[3], (DEPTH, D_MODEL, IN_PROJ_WIDTH), D_MODEL),
        "cq_norm": gain(ks[4], (DEPTH, MLA_Q_RANK)),
        "w_uq": w(ks[5], (DEPTH, MLA_Q_RANK, MLA_HEADS * MLA_QK_DIM), MLA_Q_RANK),
        "ckv_norm": gain(ks[6], (DEPTH, MLA_KV_RANK)),
        "w_ukv": w(ks[7], (DEPTH, MLA_KV_RANK, MLA_HEADS * (MLA_NOPE_DIM + MLA_V_DIM)), MLA_KV_RANK),
        "q_norm": gain(ks[8], (DEPTH, MLA_QK_DIM)),
        "k_norm": gain(ks[9], (DEPTH, MLA_QK_DIM)),
        "sb_out_norm": gain(ks[10], (DEPTH, SB_WIDTH)),
        "mla_out_norm": gain(ks[11], (DEPTH, MLA_WIDTH)),
        "w_o": w(ks[12], (DEPTH, MIX_WIDTH, D_MODEL), MIX_WIDTH),
        "ffn_norm": gain(ks[13], (DEPTH, D_MODEL)),
        "peer_w_q": w(ks[14], (DEPTH, D_MODEL, PEER_HEADS * PEER_KEY_DIM), D_MODEL),
        "peer_sub_keys": w(ks[15], (DEPTH, 2, PEER_HEADS, PEER_N_KEYS, PEER_HALF_DIM), PEER_HALF_DIM),
        "peer_u": w(ks[16], (DEPTH, PEER_N_EXPERTS, D_MODEL), D_MODEL),
        "peer_v": w(ks[17], (DEPTH, PEER_N_EXPERTS, D_MODEL), D_MODEL),
    }


def reference(x, positions, attn_norm, w_in, cq_norm, w_uq, ckv_norm, w_ukv, q_norm, k_norm,
              sb_out_norm, mla_out_norm, w_o, ffn_norm, peer_w_q, peer_sub_keys, peer_u, peer_v):
    b, s, _ = x.shape
    for l in range(DEPTH):
        h = rms_norm(x, attn_norm[l])
        proj = h @ w_in[l]
        sb_q, sb_k, sb_v, cq, ckv, k_pe = jnp.split(proj, IN_SPLITS, axis=-1)

        sb = stick_breaking_attention(sb_q.reshape(b, s, SB_HEADS, SB_HEAD_DIM),
                                      sb_k.reshape(b, s, SB_HEADS, SB_HEAD_DIM),
                                      sb_v.reshape(b, s, SB_HEADS, SB_HEAD_DIM))

        q = (rms_norm(cq, cq_norm[l]) @ w_uq[l]).reshape(b, s, MLA_HEADS, MLA_QK_DIM)
        kv = (rms_norm(ckv, ckv_norm[l]) @ w_ukv[l]).reshape(b, s, MLA_HEADS, MLA_NOPE_DIM + MLA_V_DIM)
        k_nope, v = kv[..., :MLA_NOPE_DIM], kv[..., MLA_NOPE_DIM:]
        k_rot = jnp.broadcast_to(k_pe[:, :, None, :], (b, s, MLA_HEADS, MLA_ROPE_DIM))
        k = jnp.concatenate([k_nope, k_rot], axis=-1)
        q = rms_norm(q, q_norm[l])
        k = rms_norm(k, k_norm[l])
        q = jnp.concatenate([q[..., :MLA_NOPE_DIM], rope(q[..., MLA_NOPE_DIM:], positions)], axis=-1)
        k = jnp.concatenate([k[..., :MLA_NOPE_DIM], rope(k[..., MLA_NOPE_DIM:], positions)], axis=-1)
        mla = chunk_causal_softmax_attention(q, k, v)

        mixed = jnp.concatenate([rms_norm(sb, sb_out_norm[l]),
                                 rms_norm(mla, mla_out_norm[l])], axis=-1)
        x = x + mixed @ w_o[l]

        x = x + peer_ffn(rms_norm(x, ffn_norm[l]), peer_w_q[l], peer_sub_keys[l],
                         peer_u[l], peer_v[l])
    return x
```

```python
import functools
import math

import jax
import jax.numpy as jnp
from jax import lax
from jax.experimental import pallas as pl
from jax.experimental.pallas import tpu as pltpu

D_MODEL = 1024
CHUNK = 64
EPS = 1e-6

SB_HEADS = 8
SB_HEAD_DIM = 64
SB_WIDTH = SB_HEADS * SB_HEAD_DIM

MLA_HEADS = 8
MLA_NOPE_DIM = 64
MLA_ROPE_DIM = 32
MLA_QK_DIM = MLA_NOPE_DIM + MLA_ROPE_DIM
MLA_V_DIM = 64
MLA_Q_RANK = 384
MLA_KV_RANK = 256
MLA_WIDTH = MLA_HEADS * MLA_V_DIM
ROPE_THETA = 10000.0
HEAD_PAD = 128
MLA_PAD_WIDTH = MLA_HEADS * HEAD_PAD

PEER_HEADS = 8
PEER_N_KEYS = 128
PEER_N_EXPERTS = PEER_N_KEYS * PEER_N_KEYS
PEER_KEY_DIM = 256
PEER_HALF_DIM = PEER_KEY_DIM // 2
PEER_TOPK = 16
PEER_PICKS = PEER_HEADS * PEER_TOPK

_C_SBQ = 0
_C_SBK = SB_WIDTH
_C_SBV = 2 * SB_WIDTH
_C_CQ = 3 * SB_WIDTH
_C_CKV = _C_CQ + MLA_Q_RANK
_C_KPE = _C_CKV + MLA_KV_RANK
IN_PAD_WIDTH = _C_KPE + HEAD_PAD

VMEM_LIMIT = 56 * 1024 * 1024

PREP_TOKENS = 512
ATT_BLOCK = 256
ROUTE_TOKENS = 256
EXPERT_TOKENS = 128
GATHER_SLOTS = 4


def _rms(x, g):
    ms = jnp.mean(x * x, axis=-1, keepdims=True)
    return x * lax.rsqrt(ms + EPS) * g


def _prep_kernel(x_ref, pos_ref, an_ref, win_ref, cqn_ref, wuq_ref, ckvn_ref,
                 wuk_ref, wuv_ref, qn_ref, kn_ref, freq_ref,
                 sbq_ref, sbk_ref, sbv_ref, q_ref, k_ref, v_ref):
    h = _rms(x_ref[...], an_ref[...]).astype(jnp.bfloat16)
    proj = jnp.dot(h, win_ref[...], preferred_element_type=jnp.float32)
    sbq_ref[...] = (proj[:, _C_SBQ:_C_SBQ + SB_WIDTH] * (SB_HEAD_DIM ** -0.5)).astype(jnp.bfloat16)
    sbk_ref[...] = proj[:, _C_SBK:_C_SBK + SB_WIDTH].astype(jnp.bfloat16)
    sbv_ref[...] = proj[:, _C_SBV:_C_SBV + SB_WIDTH].astype(jnp.bfloat16)

    cq = _rms(proj[:, _C_CQ:_C_CQ + MLA_Q_RANK], cqn_ref[...]).astype(jnp.bfloat16)
    ckv = _rms(proj[:, _C_CKV:_C_CKV + MLA_KV_RANK], ckvn_ref[...]).astype(jnp.bfloat16)
    kpe = proj[:, _C_KPE:_C_KPE + HEAD_PAD]
    qf = jnp.dot(cq, wuq_ref[...], preferred_element_type=jnp.float32)
    kf = jnp.dot(ckv, wuk_ref[...], preferred_element_type=jnp.float32)
    v_ref[...] = jnp.dot(ckv, wuv_ref[...], preferred_element_type=jnp.float32).astype(jnp.bfloat16)

    ang = pos_ref[...].astype(jnp.float32) * freq_ref[...]
    cos = jnp.cos(ang)
    sin = jnp.sin(ang)
    lane = lax.broadcasted_iota(jnp.int32, ang.shape, 1)
    half = MLA_ROPE_DIM // 2
    lo = (lane >= MLA_NOPE_DIM) & (lane < MLA_NOPE_DIM + half)
    hi = (lane >= MLA_NOPE_DIM + half) & (lane < MLA_QK_DIM)
    sin_lo = jnp.where(lo, -sin, 0.0)
    sin_hi = jnp.where(hi, sin, 0.0)

    def norm_rope(xh, g):
        ms = jnp.sum(xh * xh, axis=-1, keepdims=True) * (1.0 / MLA_QK_DIM)
        xn = xh * lax.rsqrt(ms + EPS) * g
        return (xn * cos + pltpu.roll(xn, HEAD_PAD - half, 1) * sin_lo
                + pltpu.roll(xn, half, 1) * sin_hi)

    for hd in range(MLA_HEADS):
        sl = slice(hd * HEAD_PAD, (hd + 1) * HEAD_PAD)
        q_ref[:, sl] = norm_rope(qf[:, sl], qn_ref[...]).astype(jnp.bfloat16)
        k_ref[:, sl] = norm_rope(kf[:, sl] + kpe, kn_ref[...]).astype(jnp.bfloat16)


def _prep_call(x2d, pos, an, win, cqn, wuq, ckvn, wuk, wuv, qn, kn, freq):
    n = x2d.shape[0]
    t = PREP_TOKENS
    full = lambda a: pl.BlockSpec(a.shape, lambda i: (0,) * a.ndim)
    tok = lambda w: pl.BlockSpec((t, w), lambda i: (i, 0))
    bf = jnp.bfloat16
    return pl.pallas_call(
        _prep_kernel,
        grid=(n // t,),
        in_specs=[tok(D_MODEL), tok(1), full(an), full(win), full(cqn), full(wuq),
                  full(ckvn), full(wuk), full(wuv), full(qn), full(kn), full(freq)],
        out_specs=[tok(SB_WIDTH), tok(SB_WIDTH), tok(SB_WIDTH),
                   tok(MLA_PAD_WIDTH), tok(MLA_PAD_WIDTH), tok(MLA_WIDTH)],
        out_shape=[jax.ShapeDtypeStruct((n, SB_WIDTH), bf)] * 3
        + [jax.ShapeDtypeStruct((n, MLA_PAD_WIDTH), bf)] * 2
        + [jax.ShapeDtypeStruct((n, MLA_WIDTH), bf)],
        compiler_params=pltpu.CompilerParams(
            dimension_semantics=("parallel",), vmem_limit_bytes=VMEM_LIMIT),
        name="prep",
    )(x2d, pos, an, win, cqn, wuq, ckvn, wuk, wuv, qn, kn, freq)


def _sb_kernel(q_ref, k_ref, v_ref, o_ref):
    i = pl.program_id(1)
    tb = ATT_BLOCK
    row = lax.broadcasted_iota(jnp.int32, (tb, tb), 0)
    col = lax.broadcasted_iota(jnp.int32, (tb, tb), 1)
    strict = col < row
    upper = jnp.where(row > col, 1.0, 0.0).astype(jnp.bfloat16)

    for hd in range(SB_HEADS):
        sl = slice(hd * SB_HEAD_DIM, (hd + 1) * SB_HEAD_DIM)
        qh = q_ref[0, :, sl]

        def block(kb, carry, acc, masked):
            start = pl.multiple_of(kb * tb, tb)
            ks = k_ref[0, pl.ds(start, tb), sl]
            vs = v_ref[0, pl.ds(start, tb), sl]
            z = lax.dot_general(qh, ks, (((1,), (1,)), ((), ())),
                                preferred_element_type=jnp.float32)
            lf_all = -(jnp.maximum(z, 0.0) + jnp.log(1.0 + jnp.exp(-jnp.abs(z))))
            lf = jnp.where(strict, lf_all, 0.0) if masked else lf_all
            lf_hi = lf.astype(jnp.bfloat16)
            lf_lo = (lf - lf_hi.astype(jnp.float32)).astype(jnp.bfloat16)
            later = (jnp.dot(lf_hi, upper, preferred_element_type=jnp.float32)
                     + jnp.dot(lf_lo, upper, preferred_element_type=jnp.float32))
            a = jnp.exp(z + lf_all + later + carry)
            if masked:
                a = jnp.where(strict, a, 0.0)
            acc = acc + jnp.dot(a.astype(jnp.bfloat16), vs,
                                preferred_element_type=jnp.float32)
            carry = carry + jnp.sum(lf, axis=-1, keepdims=True)
            return carry, acc

        carry0 = jnp.zeros((tb, 1), jnp.float32)
        acc0 = jnp.zeros((tb, SB_HEAD_DIM), jnp.float32)
        carry, acc = block(i, carry0, acc0, True)

        def body(j, c):
            return block(i - 1 - j, c[0], c[1], False)

        carry, acc = lax.fori_loop(0, i, body, (carry, acc))
        o_ref[0, :, sl] = acc


def _sb_call(q, k, v):
    b, s, w = q.shape
    tb = ATT_BLOCK
    return pl.pallas_call(
        _sb_kernel,
        grid=(b, s // tb),
        in_specs=[pl.BlockSpec((1, tb, w), lambda bi, i: (bi, i, 0)),
                  pl.BlockSpec((1, s, w), lambda bi, i: (bi, 0, 0)),
                  pl.BlockSpec((1, s, w), lambda bi, i: (bi, 0, 0))],
        out_specs=pl.BlockSpec((1, tb, w), lambda bi, i: (bi, i, 0)),
        out_shape=jax.ShapeDtypeStruct((b, s, w), jnp.float32),
        compiler_params=pltpu.CompilerParams(
            dimension_semantics=("parallel", "arbitrary"), vmem_limit_bytes=VMEM_LIMIT),
        name="sb_attn",
    )(q, k, v)


def _mla_kernel(q_ref, k_ref, v_ref, o_ref):
    i = pl.program_id(1)
    tb = ATT_BLOCK
    row = lax.broadcasted_iota(jnp.int32, (tb, tb), 0)
    col = lax.broadcasted_iota(jnp.int32, (tb, tb), 1)
    allowed = (col // CHUNK) <= (row // CHUNK)
    scale = MLA_QK_DIM ** -0.5

    for hd in range(MLA_HEADS):
        qh = q_ref[0, :, hd * HEAD_PAD:(hd + 1) * HEAD_PAD]
        vsl = slice(hd * MLA_V_DIM, (hd + 1) * MLA_V_DIM)

        def scores(kb):
            start = pl.multiple_of(kb * tb, tb)
            ks = k_ref[0, pl.ds(start, tb), hd * HEAD_PAD:(hd + 1) * HEAD_PAD]
            vs = v_ref[0, pl.ds(start, tb), vsl]
            sc = lax.dot_general(qh, ks, (((1,), (1,)), ((), ())),
                                 preferred_element_type=jnp.float32) * scale
            return sc, vs

        sc, vs = scores(i)
        sc = jnp.where(allowed, sc, -jnp.inf)
        m = jnp.max(sc, axis=-1, keepdims=True)
        p = jnp.exp(sc - m)
        l = jnp.sum(p, axis=-1, keepdims=True)
        acc = jnp.dot(p.astype(jnp.bfloat16), vs, preferred_element_type=jnp.float32)

        def body(kb, c):
            m, l, acc = c
            sc, vs = scores(kb)
            m_new = jnp.maximum(m, jnp.max(sc, axis=-1, keepdims=True))
            alpha = jnp.exp(m - m_new)
            p = jnp.exp(sc - m_new)
            l = alpha * l + jnp.sum(p, axis=-1, keepdims=True)
            acc = alpha * acc + jnp.dot(p.astype(jnp.bfloat16), vs,
                                        preferred_element_type=jnp.float32)
            return m_new, l, acc

        m, l, acc = lax.fori_loop(0, i, body, (m, l, acc))
        o_ref[0, :, vsl] = acc / l


def _mla_call(q, k, v):
    b, s, wq = q.shape
    wv = v.shape[-1]
    tb = ATT_BLOCK
    return pl.pallas_call(
        _mla_kernel,
        grid=(b, s // tb),
        in_specs=[pl.BlockSpec((1, tb, wq), lambda bi, i: (bi, i, 0)),
                  pl.BlockSpec((1, s, wq), lambda bi, i: (bi, 0, 0)),
                  pl.BlockSpec((1, s, wv), lambda bi, i: (bi, 0, 0))],
        out_specs=pl.BlockSpec((1, tb, wv), lambda bi, i: (bi, i, 0)),
        out_shape=jax.ShapeDtypeStruct((b, s, wv), jnp.float32),
        compiler_params=pltpu.CompilerParams(
            dimension_semantics=("parallel", "arbitrary"), vmem_limit_bytes=VMEM_LIMIT),
        name="mla_attn",
    )(q, k, v)


def _topk_rows(a, k):
    r, t = a.shape
    rows = lax.broadcasted_iota(jnp.int32, (r, t), 0)
    krow = lax.broadcasted_iota(jnp.int32, (k, t), 0)

    def body(it, c):
        a, vals, idxs = c
        m = jnp.max(a, axis=0, keepdims=True)
        am = jnp.min(jnp.where(a == m, rows, r), axis=0, keepdims=True)
        a = jnp.where(rows == am, -jnp.inf, a)
        vals = jnp.where(krow == it, m, vals)
        idxs = jnp.where(krow == it, am, idxs)
        return a, vals, idxs

    _, vals, idxs = lax.fori_loop(
        0, k, body, (a, jnp.zeros((k, t), jnp.float32), jnp.zeros((k, t), jnp.int32)))
    return vals, idxs


def _select_rows(sel, table):
    out = jnp.zeros(sel.shape, table.dtype)
    for r in range(table.shape[0]):
        out = jnp.where(sel == r, table[r:r + 1, :], out)
    return out


def _route_kernel(x_ref, sb_ref, mla_ref, sbn_ref, mlan_ref, wo_ref, fn_ref, wq_ref,
                  keys_ref, x2_ref, hn_ref, idx_ref, gate_ref, qp_ref):
    bf = jnp.bfloat16
    sbn = _rms(sb_ref[...], sbn_ref[...]).astype(bf)
    mlan = _rms(mla_ref[...], mlan_ref[...]).astype(bf)
    attn = (jnp.dot(sbn, wo_ref[0:SB_WIDTH, :], preferred_element_type=jnp.float32)
            + jnp.dot(mlan, wo_ref[SB_WIDTH:SB_WIDTH + MLA_WIDTH, :],
                      preferred_element_type=jnp.float32))
    x2 = x_ref[...] + attn
    x2_ref[...] = x2
    hn = _rms(x2, fn_ref[...])
    hn_ref[...] = hn
    qp = jnp.dot(hn.astype(bf), wq_ref[...], preferred_element_type=jnp.float32)
    for hc in range(2 * PEER_HEADS):
        qp_ref[hc] = qp[:, hc * PEER_HALF_DIM:(hc + 1) * PEER_HALF_DIM].astype(bf)

    k = PEER_TOPK

    def head(hd, _):
        def sub_scores(c):
            return lax.dot_general(keys_ref[2 * hd + c], qp_ref[2 * hd + c],
                                   (((1,), (1,)), ((), ())),
                                   preferred_element_type=jnp.float32)

        s0, i0 = _topk_rows(sub_scores(0), k)
        s1, i1 = _topk_rows(sub_scores(1), k)
        cand = jnp.concatenate([s0[r:r + 1, :] + s1 for r in range(k)], axis=0)
        best, pos = _topk_rows(cand, k)
        e0 = _select_rows(pos // k, i0)
        e1 = _select_rows(pos % k, i1)
        p = jnp.exp(best - best[0:1, :])
        gate = p / jnp.sum(p, axis=0, keepdims=True)
        rows = pl.ds(pl.multiple_of(hd * k, k), k)
        idx_ref[rows, :] = e0 * PEER_N_KEYS + e1
        gate_ref[rows, :] = gate
        return 0

    lax.fori_loop(0, PEER_HEADS, head, 0)


def _route_call(x2d, sb, mla, sbn, mlan, wo, fn, wq, keys):
    n = x2d.shape[0]
    t = ROUTE_TOKENS
    full = lambda a: pl.BlockSpec(a.shape, lambda i: (0,) * a.ndim)
    tok = lambda w: pl.BlockSpec((t, w), lambda i: (i, 0))
    tr = pl.BlockSpec((PEER_PICKS, t), lambda i: (0, i))
    return pl.pallas_call(
        _route_kernel,
        grid=(n // t,),
        in_specs=[tok(D_MODEL), tok(SB_WIDTH), tok(MLA_WIDTH), full(sbn), full(mlan),
                  full(wo), full(fn), full(wq), full(keys)],
        out_specs=[tok(D_MODEL), tok(D_MODEL), tr, tr],
        out_shape=[jax.ShapeDtypeStruct((n, D_MODEL), jnp.float32),
                   jax.ShapeDtypeStruct((n, D_MODEL), jnp.float32),
                   jax.ShapeDtypeStruct((PEER_PICKS, n), jnp.int32),
                   jax.ShapeDtypeStruct((PEER_PICKS, n), jnp.float32)],
        scratch_shapes=[pltpu.VMEM((2 * PEER_HEADS, t, PEER_HALF_DIM), jnp.bfloat16)],
        compiler_params=pltpu.CompilerParams(
            dimension_semantics=("parallel",), vmem_limit_bytes=VMEM_LIMIT),
        name="route",
    )(x2d, sb, mla, sbn, mlan, wo, fn, wq, keys)


def _expert_kernel(idx_hbm, uv_hbm, hn_ref, x2_ref, gate_ref, o_ref,
                   idx_smem, buf, idx_sem, row_sem):
    step = pl.program_id(0)
    tb = EXPERT_TOKENS
    ns = GATHER_SLOTS
    base = pl.multiple_of(step * tb, tb)

    idx_copy = pltpu.make_async_copy(idx_hbm.at[pl.ds(base, tb), :], idx_smem, idx_sem)
    idx_copy.start()
    idx_copy.wait()

    def row_copy(t, slot, p):
        return pltpu.make_async_copy(
            uv_hbm.at[pl.ds(idx_smem[t, p], 1), :],
            buf.at[slot, pl.ds(p, 1), :],
            row_sem.at[slot])

    def start_token(t, slot):
        for p in range(PEER_PICKS):
            row_copy(t, slot, p).start()

    def wait_token(slot):
        pltpu.make_async_copy(uv_hbm.at[pl.ds(0, PEER_PICKS), :], buf.at[slot],
                              row_sem.at[slot]).wait()

    for t in range(ns):
        start_token(t, t)

    lane = lax.broadcasted_iota(jnp.int32, (PEER_PICKS, tb), 1)

    def token(t, _):
        slot = t % ns
        wait_token(slot)
        xb = hn_ref[pl.ds(t, 1), :]
        pre = jnp.sum(buf[slot, :, 0:D_MODEL] * xb, axis=1, keepdims=True)
        gate = jnp.sum(jnp.where(lane == t, gate_ref[...], 0.0), axis=1, keepdims=True)
        coef = gate * (0.5 * pre * (1.0 + lax.erf(pre * (2.0 ** -0.5))))
        out = jnp.sum(coef * buf[slot, :, D_MODEL:2 * D_MODEL], axis=0, keepdims=True)
        o_ref[pl.ds(t, 1), :] = x2_ref[pl.ds(t, 1), :] + out

        @pl.when(t + ns < tb)
        def _():
            start_token(t + ns, slot)

        return 0

    lax.fori_loop(0, tb, token, 0)


def _expert_call(idx, uv, hn, x2, gate_t):
    n = hn.shape[0]
    tb = EXPERT_TOKENS
    tok = pl.BlockSpec((tb, D_MODEL), lambda i: (i, 0))
    return pl.pallas_call(
        _expert_kernel,
        grid=(n // tb,),
        in_specs=[pl.BlockSpec(memory_space=pl.ANY),
                  pl.BlockSpec(memory_space=pl.ANY),
                  tok, tok,
                  pl.BlockSpec((PEER_PICKS, tb), lambda i: (0, i))],
        out_specs=tok,
        out_shape=jax.ShapeDtypeStruct((n, D_MODEL), jnp.float32),
        scratch_shapes=[pltpu.SMEM((tb, PEER_PICKS), jnp.int32),
                        pltpu.VMEM((GATHER_SLOTS, PEER_PICKS, 2 * D_MODEL), jnp.float32),
                        pltpu.SemaphoreType.DMA,
                        pltpu.SemaphoreType.DMA((GATHER_SLOTS,))],
        compiler_params=pltpu.CompilerParams(
            dimension_semantics=("arbitrary",), vmem_limit_bytes=VMEM_LIMIT),
        name="experts",
    )(idx, uv, hn, x2, gate_t)


def _pad_heads(w, head_dim):
    k = w.shape[0]
    w = w.reshape(k, -1, head_dim)
    w = jnp.pad(w, ((0, 0), (0, 0), (0, HEAD_PAD - head_dim)))
    return w.reshape(k, -1)


def _layer(x2d, pos, b, s, an, w_in, cqn, w_uq, ckvn, w_ukv, qn, kn, sbn, mlan, w_o, fn,
           w_pq, sub_keys, pu, pv):
    bf = jnp.bfloat16
    row = lambda a: a.reshape(1, -1)
    kpe_cols = jnp.pad(w_in[:, _C_KPE:_C_KPE + MLA_ROPE_DIM],
                       ((0, 0), (MLA_NOPE_DIM, HEAD_PAD - MLA_QK_DIM)))
    win = jnp.concatenate([w_in[:, :_C_KPE], kpe_cols], axis=1).astype(bf)
    wuq = _pad_heads(w_uq, MLA_QK_DIM).astype(bf)
    w_ukv3 = w_ukv.reshape(MLA_KV_RANK, MLA_HEADS, MLA_NOPE_DIM + MLA_V_DIM)
    wuk = _pad_heads(w_ukv3[:, :, :MLA_NOPE_DIM].reshape(MLA_KV_RANK, -1), MLA_NOPE_DIM).astype(bf)
    wuv = w_ukv3[:, :, MLA_NOPE_DIM:].reshape(MLA_KV_RANK, -1).astype(bf)
    pad_gain = lambda g: jnp.pad(g, (0, HEAD_PAD - MLA_QK_DIM)).reshape(1, HEAD_PAD)
    half = MLA_ROPE_DIM // 2
    inv_freq = 1.0 / (ROPE_THETA ** (jnp.arange(half, dtype=jnp.float32) * (2.0 / MLA_ROPE_DIM)))
    freq = jnp.concatenate([jnp.zeros((MLA_NOPE_DIM,), jnp.float32), inv_freq, inv_freq,
                            jnp.zeros((HEAD_PAD - MLA_QK_DIM,), jnp.float32)]).reshape(1, HEAD_PAD)

    sbq, sbk, sbv, q, k, v = _prep_call(
        x2d, pos, row(an), win, row(cqn), wuq, row(ckvn), wuk, wuv,
        pad_gain(qn), pad_gain(kn), freq)

    r3 = lambda a: a.reshape(b, s, a.shape[-1])
    sb = _sb_call(r3(sbq), r3(sbk), r3(sbv)).reshape(b * s, SB_WIDTH)
    mla = _mla_call(r3(q), r3(k), r3(v)).reshape(b * s, MLA_WIDTH)

    keys = sub_keys.transpose(1, 0, 2, 3).reshape(2 * PEER_HEADS, PEER_N_KEYS, PEER_HALF_DIM).astype(bf)
    x2, hn, idx_t, gate_t = _route_call(
        x2d, sb, mla, row(sbn), row(mlan), w_o.astype(bf), row(fn), w_pq.astype(bf), keys)

    uv = jnp.concatenate([pu, pv], axis=1)
    return _expert_call(idx_t.T, uv, hn, x2, gate_t)


def kernel(x, positions, attn_norm, w_in, cq_norm, w_uq, ckv_norm, w_ukv, q_norm, k_norm,
           sb_out_norm, mla_out_norm, w_o, ffn_norm, peer_w_q, peer_sub_keys, peer_u, peer_v):
    b, s, d = x.shape
    x2d = x.reshape(b * s, d)
    pos = positions.reshape(b * s, 1)
    for l in range(attn_norm.shape[0]):
        x2d = _layer(x2d, pos, b, s, attn_norm[l], w_in[l], cq_norm[l], w_uq[l], ckv_norm[l],
                     w_ukv[l], q_norm[l], k_norm[l], sb_out_norm[l], mla_out_norm[l], w_o[l],
                     ffn_norm[l], peer_w_q[l], peer_sub_keys[l], peer_u[l], peer_v[l])
    return x2d.reshape(b, s, d)
```

```python
import functools
import math

import jax
import jax.numpy as jnp
from jax import lax
from jax.experimental import pallas as pl
from jax.experimental.pallas import tpu as pltpu

D_MODEL = 1024
CHUNK = 64
EPS = 1e-6

SB_HEADS = 8
SB_HEAD_DIM = 64
SB_WIDTH = SB_HEADS * SB_HEAD_DIM

MLA_HEADS = 8
MLA_NOPE_DIM = 64
MLA_ROPE_DIM = 32
MLA_QK_DIM = MLA_NOPE_DIM + MLA_ROPE_DIM
MLA_V_DIM = 64
MLA_Q_RANK = 384
MLA_KV_RANK = 256
MLA_WIDTH = MLA_HEADS * MLA_V_DIM
ROPE_THETA = 10000.0
HEAD_PAD = 128
MLA_PAD_WIDTH = MLA_HEADS * HEAD_PAD

PEER_HEADS = 8
PEER_N_KEYS = 128
PEER_N_EXPERTS = PEER_N_KEYS * PEER_N_KEYS
PEER_KEY_DIM = 256
PEER_HALF_DIM = PEER_KEY_DIM // 2
PEER_TOPK = 16
PEER_PICKS = PEER_HEADS * PEER_TOPK

_C_SBQ = 0
_C_SBK = SB_WIDTH
_C_SBV = 2 * SB_WIDTH
_C_CQ = 3 * SB_WIDTH
_C_CKV = _C_CQ + MLA_Q_RANK
_C_KPE = _C_CKV + MLA_KV_RANK
IN_PAD_WIDTH = _C_KPE + HEAD_PAD

VMEM_LIMIT = 56 * 1024 * 1024

PREP_TOKENS = 512
ATT_BLOCK = 256
ATT_HEAD_GROUP = 4
ROUTE_TOKENS = 256
EXPERT_TOKENS = 128
GATHER_SLOTS = 8
SLAB_ROWS = 2 * D_MODEL // 128
SLAB_PITCH = SLAB_ROWS + 1
SLOT_ROWS = PEER_PICKS * SLAB_PITCH

SB_DEAD_LOG = -105.0


def _rms(x, g):
    ms = jnp.mean(x * x, axis=-1, keepdims=True)
    return x * lax.rsqrt(ms + EPS) * g


def _prep_kernel(x_ref, pos_ref, an_ref, win_ref, cqn_ref, wuq_ref, ckvn_ref,
                 wuk_ref, wuv_ref, qn_ref, kn_ref, freq_ref,
                 sbq_ref, sbk_ref, sbv_ref, q_ref, k_ref, v_ref):
    h = _rms(x_ref[...], an_ref[...]).astype(jnp.bfloat16)
    proj = jnp.dot(h, win_ref[...], preferred_element_type=jnp.float32)
    sbq_ref[...] = (proj[:, _C_SBQ:_C_SBQ + SB_WIDTH] * (SB_HEAD_DIM ** -0.5)).astype(jnp.bfloat16)
    sbk_ref[...] = proj[:, _C_SBK:_C_SBK + SB_WIDTH].astype(jnp.bfloat16)
    sbv_ref[...] = proj[:, _C_SBV:_C_SBV + SB_WIDTH].astype(jnp.bfloat16)

    cq = _rms(proj[:, _C_CQ:_C_CQ + MLA_Q_RANK], cqn_ref[...]).astype(jnp.bfloat16)
    ckv = _rms(proj[:, _C_CKV:_C_CKV + MLA_KV_RANK], ckvn_ref[...]).astype(jnp.bfloat16)
    kpe = proj[:, _C_KPE:_C_KPE + HEAD_PAD]
    qf = jnp.dot(cq, wuq_ref[...], preferred_element_type=jnp.float32)
    kf = jnp.dot(ckv, wuk_ref[...], preferred_element_type=jnp.float32)
    v_ref[...] = jnp.dot(ckv, wuv_ref[...], preferred_element_type=jnp.float32).astype(jnp.bfloat16)

    ang = pos_ref[...].astype(jnp.float32) * freq_ref[...]
    cos = jnp.cos(ang)
    sin = jnp.sin(ang)
    lane = lax.broadcasted_iota(jnp.int32, ang.shape, 1)
    half = MLA_ROPE_DIM // 2
    lo = (lane >= MLA_NOPE_DIM) & (lane < MLA_NOPE_DIM + half)
    hi = (lane >= MLA_NOPE_DIM + half) & (lane < MLA_QK_DIM)
    sin_lo = jnp.where(lo, -sin, 0.0)
    sin_hi = jnp.where(hi, sin, 0.0)

    def norm_rope(xh, g):
        ms = jnp.sum(xh * xh, axis=-1, keepdims=True) * (1.0 / MLA_QK_DIM)
        xn = xh * lax.rsqrt(ms + EPS) * g
        return (xn * cos + pltpu.roll(xn, HEAD_PAD - half, 1) * sin_lo
                + pltpu.roll(xn, half, 1) * sin_hi)

    for hd in range(MLA_HEADS):
        sl = slice(hd * HEAD_PAD, (hd + 1) * HEAD_PAD)
        q_ref[:, sl] = norm_rope(qf[:, sl], qn_ref[...]).astype(jnp.bfloat16)
        k_ref[:, sl] = norm_rope(kf[:, sl] + kpe, kn_ref[...]).astype(jnp.bfloat16)


def _prep_call(x2d, pos, an, win, cqn, wuq, ckvn, wuk, wuv, qn, kn, freq):
    n = x2d.shape[0]
    t = PREP_TOKENS
    full = lambda a: pl.BlockSpec(a.shape, lambda i: (0,) * a.ndim)
    tok = lambda w: pl.BlockSpec((t, w), lambda i: (i, 0))
    bf = jnp.bfloat16
    return pl.pallas_call(
        _prep_kernel,
        grid=(n // t,),
        in_specs=[tok(D_MODEL), tok(1), full(an), full(win), full(cqn), full(wuq),
                  full(ckvn), full(wuk), full(wuv), full(qn), full(kn), full(freq)],
        out_specs=[tok(SB_WIDTH), tok(SB_WIDTH), tok(SB_WIDTH),
                   tok(MLA_PAD_WIDTH), tok(MLA_PAD_WIDTH), tok(MLA_WIDTH)],
        out_shape=[jax.ShapeDtypeStruct((n, SB_WIDTH), bf)] * 3
        + [jax.ShapeDtypeStruct((n, MLA_PAD_WIDTH), bf)] * 2
        + [jax.ShapeDtypeStruct((n, MLA_WIDTH), bf)],
        compiler_params=pltpu.CompilerParams(
            dimension_semantics=("parallel",), vmem_limit_bytes=VMEM_LIMIT),
        name="prep",
    )(x2d, pos, an, win, cqn, wuq, ckvn, wuk, wuv, qn, kn, freq)


def _sb_kernel(q_ref, k_ref, v_ref, o_ref):
    i = pl.program_id(1)
    tb = ATT_BLOCK
    row = lax.broadcasted_iota(jnp.int32, (tb, tb), 0)
    col = lax.broadcasted_iota(jnp.int32, (tb, tb), 1)
    strict = col < row
    upper = jnp.where(row > col, 1.0, 0.0).astype(jnp.bfloat16)

    for grp in range(SB_HEADS // ATT_HEAD_GROUP):
        heads = range(grp * ATT_HEAD_GROUP, (grp + 1) * ATT_HEAD_GROUP)

        def block(hd, kb, carry, acc, masked):
            sl = slice(hd * SB_HEAD_DIM, (hd + 1) * SB_HEAD_DIM)
            qh = q_ref[0, :, sl]
            start = pl.multiple_of(kb * tb, tb)
            ks = k_ref[0, pl.ds(start, tb), sl]
            vs = v_ref[0, pl.ds(start, tb), sl]
            z = lax.dot_general(qh, ks, (((1,), (1,)), ((), ())),
                                preferred_element_type=jnp.float32)
            lf_all = -(jnp.maximum(z, 0.0) + jnp.log(1.0 + jnp.exp(-jnp.abs(z))))
            lf = jnp.where(strict, lf_all, 0.0) if masked else lf_all
            lf_hi = lf.astype(jnp.bfloat16)
            lf_lo = (lf - lf_hi.astype(jnp.float32)).astype(jnp.bfloat16)
            later = (jnp.dot(lf_hi, upper, preferred_element_type=jnp.float32)
                     + jnp.dot(lf_lo, upper, preferred_element_type=jnp.float32))
            a = jnp.exp(z + lf_all + later + carry)
            if masked:
                a = jnp.where(strict, a, 0.0)
            acc = acc + jnp.dot(a.astype(jnp.bfloat16), vs,
                                preferred_element_type=jnp.float32)
            carry = carry + jnp.sum(lf, axis=-1, keepdims=True)
            return carry, acc

        carry0 = jnp.zeros((tb, 1), jnp.float32)
        acc0 = jnp.zeros((tb, SB_HEAD_DIM), jnp.float32)
        state = [block(hd, i, carry0, acc0, True) for hd in heads]

        def alive(st):
            top = st[0][0]
            for c, _ in st[1:]:
                top = jnp.maximum(top, c)
            return jnp.max(top) > SB_DEAD_LOG

        def cond(s):
            return (s[0] < i) & s[1]

        def body(s):
            j, _, st = s
            st = [block(hd, i - 1 - j, c, a, False) for hd, (c, a) in zip(heads, st)]
            return j + 1, alive(st), st

        _, _, state = lax.while_loop(cond, body, (0, alive(state), state))
        for hd, (_, acc) in zip(heads, state):
            o_ref[0, :, hd * SB_HEAD_DIM:(hd + 1) * SB_HEAD_DIM] = acc


def _sb_call(q, k, v):
    b, s, w = q.shape
    tb = ATT_BLOCK
    return pl.pallas_call(
        _sb_kernel,
        grid=(b, s // tb),
        in_specs=[pl.BlockSpec((1, tb, w), lambda bi, i: (bi, i, 0)),
                  pl.BlockSpec((1, s, w), lambda bi, i: (bi, 0, 0)),
                  pl.BlockSpec((1, s, w), lambda bi, i: (bi, 0, 0))],
        out_specs=pl.BlockSpec((1, tb, w), lambda bi, i: (bi, i, 0)),
        out_shape=jax.ShapeDtypeStruct((b, s, w), jnp.float32),
        compiler_params=pltpu.CompilerParams(
            dimension_semantics=("parallel", "arbitrary"), vmem_limit_bytes=VMEM_LIMIT),
        name="sb_attn",
    )(q, k, v)


def _mla_kernel(q_ref, k_ref, v_ref, o_ref):
    i = pl.program_id(1)
    tb = ATT_BLOCK
    row = lax.broadcasted_iota(jnp.int32, (tb, tb), 0)
    col = lax.broadcasted_iota(jnp.int32, (tb, tb), 1)
    allowed = (col // CHUNK) <= (row // CHUNK)
    scale = MLA_QK_DIM ** -0.5

    def scores(hd, kb):
        qh = q_ref[0, :, hd * HEAD_PAD:(hd + 1) * HEAD_PAD]
        start = pl.multiple_of(kb * tb, tb)
        ks = k_ref[0, pl.ds(start, tb), hd * HEAD_PAD:(hd + 1) * HEAD_PAD]
        vs = v_ref[0, pl.ds(start, tb), hd * MLA_V_DIM:(hd + 1) * MLA_V_DIM]
        sc = lax.dot_general(qh, ks, (((1,), (1,)), ((), ())),
                             preferred_element_type=jnp.float32) * scale
        return sc, vs

    def first(hd):
        sc, vs = scores(hd, i)
        sc = jnp.where(allowed, sc, -jnp.inf)
        m = jnp.max(sc, axis=-1, keepdims=True)
        p = jnp.exp(sc - m)
        l = jnp.sum(p, axis=-1, keepdims=True)
        acc = jnp.dot(p.astype(jnp.bfloat16), vs, preferred_element_type=jnp.float32)
        return m, l, acc

    def update(hd, kb, m, l, acc):
        sc, vs = scores(hd, kb)
        m_new = jnp.maximum(m, jnp.max(sc, axis=-1, keepdims=True))
        alpha = jnp.exp(m - m_new)
        p = jnp.exp(sc - m_new)
        l = alpha * l + jnp.sum(p, axis=-1, keepdims=True)
        acc = alpha * acc + jnp.dot(p.astype(jnp.bfloat16), vs,
                                    preferred_element_type=jnp.float32)
        return m_new, l, acc

    for grp in range(MLA_HEADS // ATT_HEAD_GROUP):
        heads = range(grp * ATT_HEAD_GROUP, (grp + 1) * ATT_HEAD_GROUP)
        state = [first(hd) for hd in heads]

        def body(kb, st):
            return [update(hd, kb, *s) for hd, s in zip(heads, st)]

        state = lax.fori_loop(0, i, body, state)
        for hd, (_, l, acc) in zip(heads, state):
            o_ref[0, :, hd * MLA_V_DIM:(hd + 1) * MLA_V_DIM] = acc / l


def _mla_call(q, k, v):
    b, s, wq = q.shape
    wv = v.shape[-1]
    tb = ATT_BLOCK
    return pl.pallas_call(
        _mla_kernel,
        grid=(b, s // tb),
        in_specs=[pl.BlockSpec((1, tb, wq), lambda bi, i: (bi, i, 0)),
                  pl.BlockSpec((1, s, wq), lambda bi, i: (bi, 0, 0)),
                  pl.BlockSpec((1, s, wv), lambda bi, i: (bi, 0, 0))],
        out_specs=pl.BlockSpec((1, tb, wv), lambda bi, i: (bi, i, 0)),
        out_shape=jax.ShapeDtypeStruct((b, s, wv), jnp.float32),
        compiler_params=pltpu.CompilerParams(
            dimension_semantics=("parallel", "arbitrary"), vmem_limit_bytes=VMEM_LIMIT),
        name="mla_attn",
    )(q, k, v)


def _topk_rows(a, k, ids=None):
    r, t = a.shape
    rows = lax.broadcasted_iota(jnp.int32, (r, t), 0) if ids is None else ids
    krow = lax.broadcasted_iota(jnp.int32, (k, t), 0)
    big = jnp.iinfo(jnp.int32).max

    def body(it, c):
        a, vals, idxs = c
        m = jnp.max(a, axis=0, keepdims=True)
        am = jnp.min(jnp.where(a == m, rows, big), axis=0, keepdims=True)
        a = jnp.where(rows == am, -jnp.inf, a)
        vals = jnp.where(krow == it, m, vals)
        idxs = jnp.where(krow == it, am, idxs)
        return a, vals, idxs

    _, vals, idxs = lax.fori_loop(
        0, k, body, (a, jnp.zeros((k, t), jnp.float32), jnp.zeros((k, t), jnp.int32)))
    return vals, idxs


def _select_rows(sel, table):
    out = jnp.zeros(sel.shape, table.dtype)
    for r in range(table.shape[0]):
        out = jnp.where(sel == r, table[r:r + 1, :], out)
    return out


def _candidate_tiles(k):
    tiles = [("row", 0, s) for s in range(0, k, 8)] + [("col", 0, s) for s in range(0, k, 8)]
    f = 1
    while (f + 1) * (f + 1) <= k:
        tiles += [("row", f, 0), ("col", f, 0)]
        f += 1
    seen, used, ids = set(), [], []
    for kind, fixed, start in tiles:
        tile_ids = []
        for r in range(8):
            pair = (fixed, start + r) if kind == "row" else (start + r, fixed)
            ok = (pair[0] + 1) * (pair[1] + 1) <= k and pair not in seen
            seen.add(pair)
            tile_ids.append(pair[0] * k + pair[1] if ok else -1)
        if max(tile_ids) >= 0:
            used.append((kind, fixed, start))
            ids += tile_ids
    want = sum(1 for a in range(k) for b in range(k) if (a + 1) * (b + 1) <= k)
    assert sum(i >= 0 for i in ids) == want, "candidate tiles must cover the whole staircase"
    return used, ids


_CAND_TILES, _CAND_IDS = _candidate_tiles(PEER_TOPK)


def _route_kernel(x_ref, sb_ref, mla_ref, sbn_ref, mlan_ref, wo_ref, fn_ref, wq_ref,
                  keys_ref, ids_ref, x2_ref, hn_ref, idx_ref, gate_ref, qp_ref):
    bf = jnp.bfloat16
    sbn = _rms(sb_ref[...], sbn_ref[...]).astype(bf)
    mlan = _rms(mla_ref[...], mlan_ref[...]).astype(bf)
    attn = (jnp.dot(sbn, wo_ref[0:SB_WIDTH, :], preferred_element_type=jnp.float32)
            + jnp.dot(mlan, wo_ref[SB_WIDTH:SB_WIDTH + MLA_WIDTH, :],
                      preferred_element_type=jnp.float32))
    x2 = x_ref[...] + attn
    x2_ref[...] = x2
    hn = _rms(x2, fn_ref[...])
    hn_ref[...] = hn
    qp = jnp.dot(hn.astype(bf), wq_ref[...], preferred_element_type=jnp.float32)
    for hc in range(2 * PEER_HEADS):
        qp_ref[hc] = qp[:, hc * PEER_HALF_DIM:(hc + 1) * PEER_HALF_DIM].astype(bf)

    k = PEER_TOPK

    def head(hd, _):
        def sub_scores(c):
            return lax.dot_general(keys_ref[2 * hd + c], qp_ref[2 * hd + c],
                                   (((1,), (1,)), ((), ())),
                                   preferred_element_type=jnp.float32)

        s0, i0 = _topk_rows(sub_scores(0), k)
        s1, i1 = _topk_rows(sub_scores(1), k)
        pieces = []
        for kind, fixed, start in _CAND_TILES:
            if kind == "row":
                pieces.append(s0[fixed:fixed + 1, :] + s1[start:start + 8, :])
            else:
                pieces.append(s0[start:start + 8, :] + s1[fixed:fixed + 1, :])
        ids = ids_ref[...]
        cand = jnp.where(ids >= 0, jnp.concatenate(pieces, axis=0), -jnp.inf)
        best, pos = _topk_rows(cand, k, ids)
        e0 = _select_rows(pos // k, i0)
        e1 = _select_rows(pos % k, i1)
        p = jnp.exp(best - best[0:1, :])
        gate = p / jnp.sum(p, axis=0, keepdims=True)
        rows = pl.ds(pl.multiple_of(hd * k, k), k)
        idx_ref[rows, :] = e0 * PEER_N_KEYS + e1
        gate_ref[rows, :] = gate
        return 0

    lax.fori_loop(0, PEER_HEADS, head, 0)


def _route_call(x2d, sb, mla, sbn, mlan, wo, fn, wq, keys):
    n = x2d.shape[0]
    t = ROUTE_TOKENS
    full = lambda a: pl.BlockSpec(a.shape, lambda i: (0,) * a.ndim)
    tok = lambda w: pl.BlockSpec((t, w), lambda i: (i, 0))
    tr = pl.BlockSpec((PEER_PICKS, t), lambda i: (0, i))
    ids = jnp.broadcast_to(jnp.asarray(_CAND_IDS, jnp.int32)[:, None], (len(_CAND_IDS), t))
    return pl.pallas_call(
        _route_kernel,
        grid=(n // t,),
        in_specs=[tok(D_MODEL), tok(SB_WIDTH), tok(MLA_WIDTH), full(sbn), full(mlan),
                  full(wo), full(fn), full(wq), full(keys), full(ids)],
        out_specs=[tok(D_MODEL), tok(D_MODEL), tr, tr],
        out_shape=[jax.ShapeDtypeStruct((n, D_MODEL), jnp.float32),
                   jax.ShapeDtypeStruct((n, D_MODEL), jnp.float32),
                   jax.ShapeDtypeStruct((PEER_PICKS, n), jnp.int32),
                   jax.ShapeDtypeStruct((PEER_PICKS, n), jnp.float32)],
        scratch_shapes=[pltpu.VMEM((2 * PEER_HEADS, t, PEER_HALF_DIM), jnp.bfloat16)],
        compiler_params=pltpu.CompilerParams(
            dimension_semantics=("parallel",), vmem_limit_bytes=VMEM_LIMIT),
        name="route",
    )(x2d, sb, mla, sbn, mlan, wo, fn, wq, keys, ids)


def _expert_kernel(idx_hbm, uv_hbm, hn_ref, x2_ref, gate_ref, o_ref,
                   idx_smem, buf, idx_sem, row_sem):
    step = pl.program_id(0)
    nsteps = pl.num_programs(0)
    tb = EXPERT_TOKENS
    ns = GATHER_SLOTS
    cur = step % 2
    has_next = step + 1 < nsteps

    def idx_copy(s, half):
        return pltpu.make_async_copy(
            idx_hbm.at[pl.ds(pl.multiple_of(s * tb, tb), tb), :], idx_smem.at[half],
            idx_sem.at[half])

    def start_token(half, t, slot):
        for p in range(PEER_PICKS):
            src = pl.multiple_of(idx_smem[half, t, p] * SLAB_ROWS, SLAB_ROWS)
            pltpu.make_async_copy(
                uv_hbm.at[pl.ds(src, SLAB_ROWS), :],
                buf.at[pl.ds(slot * SLOT_ROWS + p * SLAB_PITCH, SLAB_ROWS), :],
                row_sem.at[slot]).start(priority=p % 2)

    def wait_token(slot):
        rows = PEER_PICKS * SLAB_ROWS
        pltpu.make_async_copy(uv_hbm.at[pl.ds(0, rows), :],
                              buf.at[pl.ds(slot * SLOT_ROWS, rows), :],
                              row_sem.at[slot]).wait()

    @pl.when(step == 0)
    def _():
        idx_copy(0, 0).start()
        idx_copy(0, 0).wait()
        for t in range(ns - 1):
            start_token(0, t, t)

    @pl.when(has_next)
    def _():
        idx_copy(step + 1, 1 - cur).start()

    lane = lax.broadcasted_iota(jnp.int32, (PEER_PICKS, tb), 1)
    groups = PEER_PICKS // 8
    tiles = D_MODEL // 128

    def compute_token(t, slot):
        base = slot * SLOT_ROWS
        xrow = hn_ref[pl.ds(t, 1), :]
        pres = []
        for g in range(groups):
            acc = None
            for c in range(tiles):
                u = buf[pl.ds(base + g * 8 * SLAB_PITCH + c, 8, stride=SLAB_PITCH), :]
                term = u * xrow[:, c * 128:(c + 1) * 128]
                acc = term if acc is None else acc + term
            pres.append(jnp.sum(acc, axis=1, keepdims=True))
        pre = jnp.concatenate(pres, axis=0)
        gate = jnp.sum(jnp.where(lane == t, gate_ref[...], 0.0), axis=1, keepdims=True)
        coef = gate * (0.5 * pre * (1.0 + lax.erf(pre * (2.0 ** -0.5))))
        outs = []
        for c in range(tiles):
            acc = None
            for g in range(groups):
                v = buf[pl.ds(base + g * 8 * SLAB_PITCH + tiles + c, 8, stride=SLAB_PITCH), :]
                term = coef[g * 8:(g + 1) * 8, :] * v
                acc = term if acc is None else acc + term
            outs.append(jnp.sum(acc, axis=0, keepdims=True))
        out = jnp.concatenate(outs, axis=1)
        o_ref[pl.ds(t, 1), :] = x2_ref[pl.ds(t, 1), :] + out

    def group(q, _):
        for s in range(ns):
            t = q * ns + s
            wait_token(s)
            compute_token(t, s)
            start_token(cur, t + ns - 1, (s - 1) % ns)
        return 0

    lax.fori_loop(0, tb // ns - 1, group, 0)

    @pl.when(has_next)
    def _():
        idx_copy(step + 1, 1 - cur).wait()

    for s in range(ns):
        wait_token(s)
        compute_token(tb - ns + s, s)
        if s == 0:
            start_token(cur, tb - 1, ns - 1)
        else:
            @pl.when(has_next)
            def _():
                start_token(1 - cur, s - 1, s - 1)


def _expert_call(idx, uv, hn, x2, gate_t):
    n = hn.shape[0]
    tb = EXPERT_TOKENS
    tok = pl.BlockSpec((tb, D_MODEL), lambda i: (i, 0))
    return pl.pallas_call(
        _expert_kernel,
        grid=(n // tb,),
        in_specs=[pl.BlockSpec(memory_space=pl.ANY),
                  pl.BlockSpec(memory_space=pl.ANY),
                  tok, tok,
                  pl.BlockSpec((PEER_PICKS, tb), lambda i: (0, i))],
        out_specs=tok,
        out_shape=jax.ShapeDtypeStruct((n, D_MODEL), jnp.float32),
        scratch_shapes=[pltpu.SMEM((2, tb, PEER_PICKS), jnp.int32),
                        pltpu.VMEM((GATHER_SLOTS * SLOT_ROWS, 128), jnp.float32),
                        pltpu.SemaphoreType.DMA((2,)),
                        pltpu.SemaphoreType.DMA((GATHER_SLOTS,))],
        compiler_params=pltpu.CompilerParams(
            dimension_semantics=("arbitrary",), vmem_limit_bytes=VMEM_LIMIT),
        name="experts",
    )(idx, uv, hn, x2, gate_t)


def _pad_heads(w, head_dim):
    k = w.shape[0]
    w = w.reshape(k, -1, head_dim)
    w = jnp.pad(w, ((0, 0), (0, 0), (0, HEAD_PAD - head_dim)))
    return w.reshape(k, -1)


def _layer(x2d, pos, b, s, an, w_in, cqn, w_uq, ckvn, w_ukv, qn, kn, sbn, mlan, w_o, fn,
           w_pq, sub_keys, pu, pv):
    bf = jnp.bfloat16
    row = lambda a: a.reshape(1, -1)
    kpe_cols = jnp.pad(w_in[:, _C_KPE:_C_KPE + MLA_ROPE_DIM],
                       ((0, 0), (MLA_NOPE_DIM, HEAD_PAD - MLA_QK_DIM)))
    win = jnp.concatenate([w_in[:, :_C_KPE], kpe_cols], axis=1).astype(bf)
    wuq = _pad_heads(w_uq, MLA_QK_DIM).astype(bf)
    w_ukv3 = w_ukv.reshape(MLA_KV_RANK, MLA_HEADS, MLA_NOPE_DIM + MLA_V_DIM)
    wuk = _pad_heads(w_ukv3[:, :, :MLA_NOPE_DIM].reshape(MLA_KV_RANK, -1), MLA_NOPE_DIM).astype(bf)
    wuv = w_ukv3[:, :, MLA_NOPE_DIM:].reshape(MLA_KV_RANK, -1).astype(bf)
    pad_gain = lambda g: jnp.pad(g, (0, HEAD_PAD - MLA_QK_DIM)).reshape(1, HEAD_PAD)
    half = MLA_ROPE_DIM // 2
    inv_freq = 1.0 / (ROPE_THETA ** (jnp.arange(half, dtype=jnp.float32) * (2.0 / MLA_ROPE_DIM)))
    freq = jnp.concatenate([jnp.zeros((MLA_NOPE_DIM,), jnp.float32), inv_freq, inv_freq,
                            jnp.zeros((HEAD_PAD - MLA_QK_DIM,), jnp.float32)]).reshape(1, HEAD_PAD)

    sbq, sbk, sbv, q, k, v = _prep_call(
        x2d, pos, row(an), win, row(cqn), wuq, row(ckvn), wuk, wuv,
        pad_gain(qn), pad_gain(kn), freq)

    r3 = lambda a: a.reshape(b, s, a.shape[-1])
    sb = _sb_call(r3(sbq), r3(sbk), r3(sbv)).reshape(b * s, SB_WIDTH)
    mla = _mla_call(r3(q), r3(k), r3(v)).reshape(b * s, MLA_WIDTH)

    keys = sub_keys.transpose(1, 0, 2, 3).reshape(2 * PEER_HEADS, PEER_N_KEYS, PEER_HALF_DIM).astype(bf)
    x2, hn, idx_t, gate_t = _route_call(
        x2d, sb, mla, row(sbn), row(mlan), w_o.astype(bf), row(fn), w_pq.astype(bf), keys)

    slab = lambda w: w.reshape(PEER_N_EXPERTS, D_MODEL // 128, 128)
    uv = jnp.concatenate([slab(pu), slab(pv)], axis=1).reshape(PEER_N_EXPERTS * SLAB_ROWS, 128)
    return _expert_call(idx_t.T, uv, hn, x2, gate_t)


def kernel(x, positions, attn_norm, w_in, cq_norm, w_uq, ckv_norm, w_ukv, q_norm, k_norm,
           sb_out_norm, mla_out_norm, w_o, ffn_norm, peer_w_q, peer_sub_keys, peer_u, peer_v):
    b, s, d = x.shape
    x2d = x.reshape(b * s, d)
    pos = positions.reshape(b * s, 1)
    for l in range(attn_norm.shape[0]):
        x2d = _layer(x2d, pos, b, s, attn_norm[l], w_in[l], cq_norm[l], w_uq[l], ckv_norm[l],
                     w_ukv[l], q_norm[l], k_norm[l], sb_out_norm[l], mla_out_norm[l], w_o[l],
                     ffn_norm[l], peer_w_q[l], peer_sub_keys[l], peer_u[l], peer_v[l])
    return x2d.reshape(b, s, d)
```

```python
import functools
import math

import jax
import jax.numpy as jnp
from jax import lax
from jax.experimental import pallas as pl
from jax.experimental.pallas import tpu as pltpu

D_MODEL = 1024
CHUNK = 64
EPS = 1e-6

SB_HEADS = 8
SB_HEAD_DIM = 64
SB_WIDTH = SB_HEADS * SB_HEAD_DIM

MLA_HEADS = 8
MLA_NOPE_DIM = 64
MLA_ROPE_DIM = 32
MLA_QK_DIM = MLA_NOPE_DIM + MLA_ROPE_DIM
MLA_V_DIM = 64
MLA_Q_RANK = 384
MLA_KV_RANK = 256
MLA_WIDTH = MLA_HEADS * MLA_V_DIM
ROPE_THETA = 10000.0
HEAD_PAD = 128
MLA_PAD_WIDTH = MLA_HEADS * HEAD_PAD

PEER_HEADS = 8
PEER_N_KEYS = 128
PEER_N_EXPERTS = PEER_N_KEYS * PEER_N_KEYS
PEER_KEY_DIM = 256
PEER_HALF_DIM = PEER_KEY_DIM // 2
PEER_TOPK = 16
PEER_PICKS = PEER_HEADS * PEER_TOPK

_C_SBQ = 0
_C_SBK = SB_WIDTH
_C_SBV = 2 * SB_WIDTH
_C_CQ = 3 * SB_WIDTH
_C_CKV = _C_CQ + MLA_Q_RANK
_C_KPE = _C_CKV + MLA_KV_RANK
IN_PAD_WIDTH = _C_KPE + HEAD_PAD

VMEM_LIMIT = 56 * 1024 * 1024

PREP_TOKENS = 512
ATT_BLOCK = 256
ATT_HEAD_GROUP = 4
MLA_HEAD_GROUP = 8
ROUTE_TOKENS = 256
EXPERT_TOKENS = 128
GATHER_SLOTS = 8
SLAB_ROWS = 2 * D_MODEL // 128
SLAB_PITCH = SLAB_ROWS + 1
SLOT_ROWS = PEER_PICKS * SLAB_PITCH

SB_DEAD_LOG = -105.0


def _rms(x, g):
    ms = jnp.mean(x * x, axis=-1, keepdims=True)
    return x * lax.rsqrt(ms + EPS) * g


def _prep_kernel(x_ref, pos_ref, an_ref, win_ref, cqn_ref, wuq_ref, ckvn_ref,
                 wuk_ref, wuv_ref, qn_ref, kn_ref, freq_ref,
                 sbq_ref, sbk_ref, sbv_ref, q_ref, k_ref, v_ref):
    h = _rms(x_ref[...], an_ref[...]).astype(jnp.bfloat16)
    proj = jnp.dot(h, win_ref[...], preferred_element_type=jnp.float32)
    sbq_ref[...] = (proj[:, _C_SBQ:_C_SBQ + SB_WIDTH] * (SB_HEAD_DIM ** -0.5)).astype(jnp.bfloat16)
    sbk_ref[...] = proj[:, _C_SBK:_C_SBK + SB_WIDTH].astype(jnp.bfloat16)
    sbv_ref[...] = proj[:, _C_SBV:_C_SBV + SB_WIDTH].astype(jnp.bfloat16)

    cq = _rms(proj[:, _C_CQ:_C_CQ + MLA_Q_RANK], cqn_ref[...]).astype(jnp.bfloat16)
    ckv = _rms(proj[:, _C_CKV:_C_CKV + MLA_KV_RANK], ckvn_ref[...]).astype(jnp.bfloat16)
    kpe = proj[:, _C_KPE:_C_KPE + HEAD_PAD]
    qf = jnp.dot(cq, wuq_ref[...], preferred_element_type=jnp.float32)
    kf = jnp.dot(ckv, wuk_ref[...], preferred_element_type=jnp.float32)
    v_ref[...] = jnp.dot(ckv, wuv_ref[...], preferred_element_type=jnp.float32).astype(jnp.bfloat16)

    ang = pos_ref[...].astype(jnp.float32) * freq_ref[...]
    cos = jnp.cos(ang)
    sin = jnp.sin(ang)
    lane = lax.broadcasted_iota(jnp.int32, ang.shape, 1)
    half = MLA_ROPE_DIM // 2
    lo = (lane >= MLA_NOPE_DIM) & (lane < MLA_NOPE_DIM + half)
    hi = (lane >= MLA_NOPE_DIM + half) & (lane < MLA_QK_DIM)
    sin_lo = jnp.where(lo, -sin, 0.0)
    sin_hi = jnp.where(hi, sin, 0.0)

    def norm_rope(xh, g):
        ms = jnp.sum(xh * xh, axis=-1, keepdims=True) * (1.0 / MLA_QK_DIM)
        xn = xh * lax.rsqrt(ms + EPS) * g
        return (xn * cos + pltpu.roll(xn, HEAD_PAD - half, 1) * sin_lo
                + pltpu.roll(xn, half, 1) * sin_hi)

    for hd in range(MLA_HEADS):
        sl = slice(hd * HEAD_PAD, (hd + 1) * HEAD_PAD)
        q_ref[:, sl] = norm_rope(qf[:, sl], qn_ref[...]).astype(jnp.bfloat16)
        k_ref[:, sl] = norm_rope(kf[:, sl] + kpe, kn_ref[...]).astype(jnp.bfloat16)


def _prep_call(x2d, pos, an, win, cqn, wuq, ckvn, wuk, wuv, qn, kn, freq):
    n = x2d.shape[0]
    t = PREP_TOKENS
    full = lambda a: pl.BlockSpec(a.shape, lambda i: (0,) * a.ndim)
    tok = lambda w: pl.BlockSpec((t, w), lambda i: (i, 0))
    bf = jnp.bfloat16
    return pl.pallas_call(
        _prep_kernel,
        grid=(n // t,),
        in_specs=[tok(D_MODEL), tok(1), full(an), full(win), full(cqn), full(wuq),
                  full(ckvn), full(wuk), full(wuv), full(qn), full(kn), full(freq)],
        out_specs=[tok(SB_WIDTH), tok(SB_WIDTH), tok(SB_WIDTH),
                   tok(MLA_PAD_WIDTH), tok(MLA_PAD_WIDTH), tok(MLA_WIDTH)],
        out_shape=[jax.ShapeDtypeStruct((n, SB_WIDTH), bf)] * 3
        + [jax.ShapeDtypeStruct((n, MLA_PAD_WIDTH), bf)] * 2
        + [jax.ShapeDtypeStruct((n, MLA_WIDTH), bf)],
        compiler_params=pltpu.CompilerParams(
            dimension_semantics=("parallel",), vmem_limit_bytes=VMEM_LIMIT),
        name="prep",
    )(x2d, pos, an, win, cqn, wuq, ckvn, wuk, wuv, qn, kn, freq)


def _sb_kernel(q_ref, k_ref, v_ref, o_ref):
    i = pl.program_id(1)
    tb = ATT_BLOCK
    row = lax.broadcasted_iota(jnp.int32, (tb, tb), 0)
    col = lax.broadcasted_iota(jnp.int32, (tb, tb), 1)
    strict = col < row
    upper = jnp.where(row > col, 1.0, 0.0).astype(jnp.bfloat16)

    for grp in range(SB_HEADS // ATT_HEAD_GROUP):
        heads = range(grp * ATT_HEAD_GROUP, (grp + 1) * ATT_HEAD_GROUP)

        def block(hd, kb, carry, acc, masked):
            sl = slice(hd * SB_HEAD_DIM, (hd + 1) * SB_HEAD_DIM)
            qh = q_ref[0, :, sl]
            start = pl.multiple_of(kb * tb, tb)
            ks = k_ref[0, pl.ds(start, tb), sl]
            vs = v_ref[0, pl.ds(start, tb), sl]
            z = lax.dot_general(qh, ks, (((1,), (1,)), ((), ())),
                                preferred_element_type=jnp.float32)
            lf_all = -(jnp.maximum(z, 0.0) + jnp.log(1.0 + jnp.exp(-jnp.abs(z))))
            lf = jnp.where(strict, lf_all, 0.0) if masked else lf_all
            lf_hi = lf.astype(jnp.bfloat16)
            lf_lo = (lf - lf_hi.astype(jnp.float32)).astype(jnp.bfloat16)
            later = (jnp.dot(lf_hi, upper, preferred_element_type=jnp.float32)
                     + jnp.dot(lf_lo, upper, preferred_element_type=jnp.float32))
            a = jnp.exp(z + lf_all + later + carry)
            if masked:
                a = jnp.where(strict, a, 0.0)
            acc = acc + jnp.dot(a.astype(jnp.bfloat16), vs,
                                preferred_element_type=jnp.float32)
            carry = carry + jnp.sum(lf, axis=-1, keepdims=True)
            return carry, acc

        carry0 = jnp.zeros((tb, 1), jnp.float32)
        acc0 = jnp.zeros((tb, SB_HEAD_DIM), jnp.float32)
        state = [block(hd, i, carry0, acc0, True) for hd in heads]

        def alive(st):
            top = st[0][0]
            for c, _ in st[1:]:
                top = jnp.maximum(top, c)
            return jnp.max(top) > SB_DEAD_LOG

        def cond(s):
            return (s[0] < i) & s[1]

        def body(s):
            j, _, st = s
            st = [block(hd, i - 1 - j, c, a, False) for hd, (c, a) in zip(heads, st)]
            return j + 1, alive(st), st

        _, _, state = lax.while_loop(cond, body, (0, alive(state), state))
        for hd, (_, acc) in zip(heads, state):
            o_ref[0, :, hd * SB_HEAD_DIM:(hd + 1) * SB_HEAD_DIM] = acc


def _sb_call(q, k, v):
    b, s, w = q.shape
    tb = ATT_BLOCK
    return pl.pallas_call(
        _sb_kernel,
        grid=(b, s // tb),
        in_specs=[pl.BlockSpec((1, tb, w), lambda bi, i: (bi, i, 0)),
                  pl.BlockSpec((1, s, w), lambda bi, i: (bi, 0, 0)),
                  pl.BlockSpec((1, s, w), lambda bi, i: (bi, 0, 0))],
        out_specs=pl.BlockSpec((1, tb, w), lambda bi, i: (bi, i, 0)),
        out_shape=jax.ShapeDtypeStruct((b, s, w), jnp.float32),
        compiler_params=pltpu.CompilerParams(
            dimension_semantics=("parallel", "arbitrary"), vmem_limit_bytes=VMEM_LIMIT),
        name="sb_attn",
    )(q, k, v)


def _mla_kernel(q_ref, k_ref, v_ref, o_ref):
    i = pl.program_id(1)
    tb = ATT_BLOCK
    row = lax.broadcasted_iota(jnp.int32, (tb, tb), 0)
    col = lax.broadcasted_iota(jnp.int32, (tb, tb), 1)
    allowed = (col // CHUNK) <= (row // CHUNK)
    scale = MLA_QK_DIM ** -0.5

    def scores(hd, kb):
        qh = q_ref[0, :, hd * HEAD_PAD:(hd + 1) * HEAD_PAD]
        start = pl.multiple_of(kb * tb, tb)
        ks = k_ref[0, pl.ds(start, tb), hd * HEAD_PAD:(hd + 1) * HEAD_PAD]
        vs = v_ref[0, pl.ds(start, tb), hd * MLA_V_DIM:(hd + 1) * MLA_V_DIM]
        sc = lax.dot_general(qh, ks, (((1,), (1,)), ((), ())),
                             preferred_element_type=jnp.float32) * scale
        return sc, vs

    def first(hd):
        sc, vs = scores(hd, i)
        sc = jnp.where(allowed, sc, -jnp.inf)
        m = jnp.max(sc, axis=-1, keepdims=True)
        p = jnp.exp(sc - m)
        l = jnp.sum(p, axis=-1, keepdims=True)
        acc = jnp.dot(p.astype(jnp.bfloat16), vs, preferred_element_type=jnp.float32)
        return m, l, acc

    def update(hd, kb, m, l, acc):
        sc, vs = scores(hd, kb)
        m_new = jnp.maximum(m, jnp.max(sc, axis=-1, keepdims=True))
        alpha = jnp.exp(m - m_new)
        p = jnp.exp(sc - m_new)
        l = alpha * l + jnp.sum(p, axis=-1, keepdims=True)
        acc = alpha * acc + jnp.dot(p.astype(jnp.bfloat16), vs,
                                    preferred_element_type=jnp.float32)
        return m_new, l, acc

    for grp in range(MLA_HEADS // MLA_HEAD_GROUP):
        heads = range(grp * MLA_HEAD_GROUP, (grp + 1) * MLA_HEAD_GROUP)
        state = [first(hd) for hd in heads]

        def body(kb, st):
            return [update(hd, kb, *s) for hd, s in zip(heads, st)]

        state = lax.fori_loop(0, i, body, state)
        for hd, (_, l, acc) in zip(heads, state):
            o_ref[0, :, hd * MLA_V_DIM:(hd + 1) * MLA_V_DIM] = acc / l


def _mla_call(q, k, v):
    b, s, wq = q.shape
    wv = v.shape[-1]
    tb = ATT_BLOCK
    return pl.pallas_call(
        _mla_kernel,
        grid=(b, s // tb),
        in_specs=[pl.BlockSpec((1, tb, wq), lambda bi, i: (bi, i, 0)),
                  pl.BlockSpec((1, s, wq), lambda bi, i: (bi, 0, 0)),
                  pl.BlockSpec((1, s, wv), lambda bi, i: (bi, 0, 0))],
        out_specs=pl.BlockSpec((1, tb, wv), lambda bi, i: (bi, i, 0)),
        out_shape=jax.ShapeDtypeStruct((b, s, wv), jnp.float32),
        compiler_params=pltpu.CompilerParams(
            dimension_semantics=("parallel", "arbitrary"), vmem_limit_bytes=VMEM_LIMIT),
        name="mla_attn",
    )(q, k, v)


def _topk_rows(a, k, ids=None):
    r, t = a.shape
    rows = lax.broadcasted_iota(jnp.int32, (r, t), 0) if ids is None else ids
    krow = lax.broadcasted_iota(jnp.int32, (k, t), 0)
    big = jnp.iinfo(jnp.int32).max

    def body(it, c):
        a, vals, idxs = c
        m = jnp.max(a, axis=0, keepdims=True)
        am = jnp.min(jnp.where(a == m, rows, big), axis=0, keepdims=True)
        a = jnp.where(rows == am, -jnp.inf, a)
        vals = jnp.where(krow == it, m, vals)
        idxs = jnp.where(krow == it, am, idxs)
        return a, vals, idxs

    _, vals, idxs = lax.fori_loop(
        0, k, body, (a, jnp.zeros((k, t), jnp.float32), jnp.zeros((k, t), jnp.int32)))
    return vals, idxs


def _select_rows(sel, table):
    out = jnp.zeros(sel.shape, table.dtype)
    for r in range(table.shape[0]):
        out = jnp.where(sel == r, table[r:r + 1, :], out)
    return out


def _candidate_tiles(k):
    tiles = [("row", 0, s) for s in range(0, k, 8)] + [("col", 0, s) for s in range(0, k, 8)]
    f = 1
    while (f + 1) * (f + 1) <= k:
        tiles += [("row", f, 0), ("col", f, 0)]
        f += 1
    seen, used, ids = set(), [], []
    for kind, fixed, start in tiles:
        tile_ids = []
        for r in range(8):
            pair = (fixed, start + r) if kind == "row" else (start + r, fixed)
            ok = (pair[0] + 1) * (pair[1] + 1) <= k and pair not in seen
            seen.add(pair)
            tile_ids.append(pair[0] * k + pair[1] if ok else -1)
        if max(tile_ids) >= 0:
            used.append((kind, fixed, start))
            ids += tile_ids
    want = sum(1 for a in range(k) for b in range(k) if (a + 1) * (b + 1) <= k)
    assert sum(i >= 0 for i in ids) == want, "candidate tiles must cover the whole staircase"
    return used, ids


_CAND_TILES, _CAND_IDS = _candidate_tiles(PEER_TOPK)


def _route_kernel(x_ref, sb_ref, mla_ref, sbn_ref, mlan_ref, wo_ref, fn_ref, wq_ref,
                  keys_ref, ids_ref, x2_ref, hn_ref, idx_ref, gate_ref, qp_ref):
    bf = jnp.bfloat16
    sbn = _rms(sb_ref[...], sbn_ref[...]).astype(bf)
    mlan = _rms(mla_ref[...], mlan_ref[...]).astype(bf)
    attn = (jnp.dot(sbn, wo_ref[0:SB_WIDTH, :], preferred_element_type=jnp.float32)
            + jnp.dot(mlan, wo_ref[SB_WIDTH:SB_WIDTH + MLA_WIDTH, :],
                      preferred_element_type=jnp.float32))
    x2 = x_ref[...] + attn
    x2_ref[...] = x2
    hn = _rms(x2, fn_ref[...])
    hn_ref[...] = hn
    qp = jnp.dot(hn.astype(bf), wq_ref[...], preferred_element_type=jnp.float32)
    for hc in range(2 * PEER_HEADS):
        qp_ref[hc] = qp[:, hc * PEER_HALF_DIM:(hc + 1) * PEER_HALF_DIM].astype(bf)

    k = PEER_TOPK

    def head(hd, _):
        def sub_scores(c):
            return lax.dot_general(keys_ref[2 * hd + c], qp_ref[2 * hd + c],
                                   (((1,), (1,)), ((), ())),
                                   preferred_element_type=jnp.float32)

        s0, i0 = _topk_rows(sub_scores(0), k)
        s1, i1 = _topk_rows(sub_scores(1), k)
        pieces = []
        for kind, fixed, start in _CAND_TILES:
            if kind == "row":
                pieces.append(s0[fixed:fixed + 1, :] + s1[start:start + 8, :])
            else:
                pieces.append(s0[start:start + 8, :] + s1[fixed:fixed + 1, :])
        ids = ids_ref[...]
        cand = jnp.where(ids >= 0, jnp.concatenate(pieces, axis=0), -jnp.inf)
        best, pos = _topk_rows(cand, k, ids)
        e0 = _select_rows(pos // k, i0)
        e1 = _select_rows(pos % k, i1)
        p = jnp.exp(best - best[0:1, :])
        gate = p / jnp.sum(p, axis=0, keepdims=True)
        rows = pl.ds(pl.multiple_of(hd * k, k), k)
        idx_ref[rows, :] = e0 * PEER_N_KEYS + e1
        gate_ref[rows, :] = gate
        return 0

    lax.fori_loop(0, PEER_HEADS, head, 0)


def _route_call(x2d, sb, mla, sbn, mlan, wo, fn, wq, keys):
    n = x2d.shape[0]
    t = ROUTE_TOKENS
    full = lambda a: pl.BlockSpec(a.shape, lambda i: (0,) * a.ndim)
    tok = lambda w: pl.BlockSpec((t, w), lambda i: (i, 0))
    tr = pl.BlockSpec((PEER_PICKS, t), lambda i: (0, i))
    ids = jnp.broadcast_to(jnp.asarray(_CAND_IDS, jnp.int32)[:, None], (len(_CAND_IDS), t))
    return pl.pallas_call(
        _route_kernel,
        grid=(n // t,),
        in_specs=[tok(D_MODEL), tok(SB_WIDTH), tok(MLA_WIDTH), full(sbn), full(mlan),
                  full(wo), full(fn), full(wq), full(keys), full(ids)],
        out_specs=[tok(D_MODEL), tok(D_MODEL), tr, tr],
        out_shape=[jax.ShapeDtypeStruct((n, D_MODEL), jnp.float32),
                   jax.ShapeDtypeStruct((n, D_MODEL), jnp.float32),
                   jax.ShapeDtypeStruct((PEER_PICKS, n), jnp.int32),
                   jax.ShapeDtypeStruct((PEER_PICKS, n), jnp.float32)],
        scratch_shapes=[pltpu.VMEM((2 * PEER_HEADS, t, PEER_HALF_DIM), jnp.bfloat16)],
        compiler_params=pltpu.CompilerParams(
            dimension_semantics=("parallel",), vmem_limit_bytes=VMEM_LIMIT),
        name="route",
    )(x2d, sb, mla, sbn, mlan, wo, fn, wq, keys, ids)


def _expert_kernel(idx_hbm, uv_hbm, hn_ref, x2_ref, gate_ref, o_ref,
                   idx_smem, buf, idx_sem, row_sem):
    step = pl.program_id(0)
    nsteps = pl.num_programs(0)
    tb = EXPERT_TOKENS
    ns = GATHER_SLOTS
    cur = step % 2
    has_next = step + 1 < nsteps

    ring = 2 * tb

    def idx_copy(s, half):
        return pltpu.make_async_copy(
            idx_hbm.at[pl.ds(pl.multiple_of(s * tb, tb), tb), :],
            idx_smem.at[pl.ds(pl.multiple_of(half * tb, tb), tb), :],
            idx_sem.at[half])

    def start_token(ring_row, slot):
        base = slot * SLOT_ROWS
        for p in range(PEER_PICKS):
            src = pl.multiple_of(idx_smem[ring_row, p] * SLAB_ROWS, SLAB_ROWS)
            pltpu.make_async_copy(
                uv_hbm.at[pl.ds(src, SLAB_ROWS), :],
                buf.at[pl.ds(base + p * SLAB_PITCH, SLAB_ROWS), :],
                row_sem.at[slot]).start(priority=p % 2)

    def wait_token(slot):
        rows = PEER_PICKS * SLAB_ROWS
        pltpu.make_async_copy(uv_hbm.at[pl.ds(0, rows), :],
                              buf.at[pl.ds(slot * SLOT_ROWS, rows), :],
                              row_sem.at[slot]).wait()

    @pl.when(step == 0)
    def _():
        idx_copy(0, 0).start()
        idx_copy(0, 0).wait()
        for t in range(ns - 1):
            start_token(t, t)

    @pl.when(has_next)
    def _():
        idx_copy(step + 1, 1 - cur).start()

    lane = lax.broadcasted_iota(jnp.int32, (PEER_PICKS, tb), 1)
    groups = PEER_PICKS // 8
    tiles = D_MODEL // 128

    def compute_token(t, slot):
        base = slot * SLOT_ROWS
        xrow = hn_ref[pl.ds(t, 1), :]
        pres = []
        for g in range(groups):
            acc = None
            for c in range(tiles):
                u = buf[pl.ds(base + g * 8 * SLAB_PITCH + c, 8, stride=SLAB_PITCH), :]
                term = u * xrow[:, c * 128:(c + 1) * 128]
                acc = term if acc is None else acc + term
            pres.append(jnp.sum(acc, axis=1, keepdims=True))
        pre = jnp.concatenate(pres, axis=0)
        gate = jnp.sum(jnp.where(lane == t, gate_ref[...], 0.0), axis=1, keepdims=True)
        coef = gate * (0.5 * pre * (1.0 + lax.erf(pre * (2.0 ** -0.5))))
        outs = []
        for c in range(tiles):
            acc = None
            for g in range(groups):
                v = buf[pl.ds(base + g * 8 * SLAB_PITCH + tiles + c, 8, stride=SLAB_PITCH), :]
                term = coef[g * 8:(g + 1) * 8, :] * v
                acc = term if acc is None else acc + term
            outs.append(jnp.sum(acc, axis=0, keepdims=True))
        out = jnp.concatenate(outs, axis=1)
        o_ref[pl.ds(t, 1), :] = x2_ref[pl.ds(t, 1), :] + out

    def ahead(t):
        return (cur * tb + t + ns - 1) % ring, (t + ns - 1) % ns

    def pair(i, _):
        t = 2 * i
        wait_token(t % ns)
        wait_token((t + 1) % ns)
        compute_token(t, t % ns)
        compute_token(t + 1, (t + 1) % ns)
        start_token(*ahead(t))
        start_token(*ahead(t + 1))
        return 0

    lax.fori_loop(0, (tb - ns) // 2, pair, 0)

    @pl.when(has_next)
    def _():
        idx_copy(step + 1, 1 - cur).wait()

    def tail_token(t, _):
        slot = t % ns
        wait_token(slot)
        compute_token(t, slot)

        @pl.when(has_next | (t + ns - 1 < tb))
        def _():
            start_token(*ahead(t))

        return 0

    lax.fori_loop(tb - ns, tb, tail_token, 0)


def _expert_call(idx, uv, hn, x2, gate_t):
    n = hn.shape[0]
    tb = EXPERT_TOKENS
    tok = pl.BlockSpec((tb, D_MODEL), lambda i: (i, 0))
    return pl.pallas_call(
        _expert_kernel,
        grid=(n // tb,),
        in_specs=[pl.BlockSpec(memory_space=pl.ANY),
                  pl.BlockSpec(memory_space=pl.ANY),
                  tok, tok,
                  pl.BlockSpec((PEER_PICKS, tb), lambda i: (0, i))],
        out_specs=tok,
        out_shape=jax.ShapeDtypeStruct((n, D_MODEL), jnp.float32),
        scratch_shapes=[pltpu.SMEM((2 * tb, PEER_PICKS), jnp.int32),
                        pltpu.VMEM((GATHER_SLOTS * SLOT_ROWS, 128), jnp.float32),
                        pltpu.SemaphoreType.DMA((2,)),
                        pltpu.SemaphoreType.DMA((GATHER_SLOTS,))],
        compiler_params=pltpu.CompilerParams(
            dimension_semantics=("arbitrary",), vmem_limit_bytes=VMEM_LIMIT),
        name="experts",
    )(idx, uv, hn, x2, gate_t)


def _pad_heads(w, head_dim):
    k = w.shape[0]
    w = w.reshape(k, -1, head_dim)
    w = jnp.pad(w, ((0, 0), (0, 0), (0, HEAD_PAD - head_dim)))
    return w.reshape(k, -1)


def _layer(x2d, pos, b, s, an, w_in, cqn, w_uq, ckvn, w_ukv, qn, kn, sbn, mlan, w_o, fn,
           w_pq, sub_keys, pu, pv):
    bf = jnp.bfloat16
    row = lambda a: a.reshape(1, -1)
    kpe_cols = jnp.pad(w_in[:, _C_KPE:_C_KPE + MLA_ROPE_DIM],
                       ((0, 0), (MLA_NOPE_DIM, HEAD_PAD - MLA_QK_DIM)))
    win = jnp.concatenate([w_in[:, :_C_KPE], kpe_cols], axis=1).astype(bf)
    wuq = _pad_heads(w_uq, MLA_QK_DIM).astype(bf)
    w_ukv3 = w_ukv.reshape(MLA_KV_RANK, MLA_HEADS, MLA_NOPE_DIM + MLA_V_DIM)
    wuk = _pad_heads(w_ukv3[:, :, :MLA_NOPE_DIM].reshape(MLA_KV_RANK, -1), MLA_NOPE_DIM).astype(bf)
    wuv = w_ukv3[:, :, MLA_NOPE_DIM:].reshape(MLA_KV_RANK, -1).astype(bf)
    pad_gain = lambda g: jnp.pad(g, (0, HEAD_PAD - MLA_QK_DIM)).reshape(1, HEAD_PAD)
    half = MLA_ROPE_DIM // 2
    inv_freq = 1.0 / (ROPE_THETA ** (jnp.arange(half, dtype=jnp.float32) * (2.0 / MLA_ROPE_DIM)))
    freq = jnp.concatenate([jnp.zeros((MLA_NOPE_DIM,), jnp.float32), inv_freq, inv_freq,
                            jnp.zeros((HEAD_PAD - MLA_QK_DIM,), jnp.float32)]).reshape(1, HEAD_PAD)

    sbq, sbk, sbv, q, k, v = _prep_call(
        x2d, pos, row(an), win, row(cqn), wuq, row(ckvn), wuk, wuv,
        pad_gain(qn), pad_gain(kn), freq)

    r3 = lambda a: a.reshape(b, s, a.shape[-1])
    sb = _sb_call(r3(sbq), r3(sbk), r3(sbv)).reshape(b * s, SB_WIDTH)
    mla = _mla_call(r3(q), r3(k), r3(v)).reshape(b * s, MLA_WIDTH)

    keys = sub_keys.transpose(1, 0, 2, 3).reshape(2 * PEER_HEADS, PEER_N_KEYS, PEER_HALF_DIM).astype(bf)
    x2, hn, idx_t, gate_t = _route_call(
        x2d, sb, mla, row(sbn), row(mlan), w_o.astype(bf), row(fn), w_pq.astype(bf), keys)

    slab = lambda w: w.reshape(PEER_N_EXPERTS, D_MODEL // 128, 128)
    uv = jnp.concatenate([slab(pu), slab(pv)], axis=1).reshape(PEER_N_EXPERTS * SLAB_ROWS, 128)
    return _expert_call(idx_t.T, uv, hn, x2, gate_t)


def kernel(x, positions, attn_norm, w_in, cq_norm, w_uq, ckv_norm, w_ukv, q_norm, k_norm,
           sb_out_norm, mla_out_norm, w_o, ffn_norm, peer_w_q, peer_sub_keys, peer_u, peer_v):
    b, s, d = x.shape
    x2d = x.reshape(b * s, d)
    pos = positions.reshape(b * s, 1)
    for l in range(attn_norm.shape[0]):
        x2d = _layer(x2d, pos, b, s, attn_norm[l], w_in[l], cq_norm[l], w_uq[l], ckv_norm[l],
                     w_ukv[l], q_norm[l], k_norm[l], sb_out_norm[l], mla_out_norm[l], w_o[l],
                     ffn_norm[l], peer_w_q[l], peer_sub_keys[l], peer_u[l], peer_v[l])
    return x2d.reshape(b, s, d)
```

```python
import functools
import math

import jax
import jax.numpy as jnp
from jax import lax
from jax.experimental import pallas as pl
from jax.experimental.pallas import tpu as pltpu

D_MODEL = 1024
CHUNK = 64
EPS = 1e-6

SB_HEADS = 8
SB_HEAD_DIM = 64
SB_WIDTH = SB_HEADS * SB_HEAD_DIM

MLA_HEADS = 8
MLA_NOPE_DIM = 64
MLA_ROPE_DIM = 32
MLA_QK_DIM = MLA_NOPE_DIM + MLA_ROPE_DIM
MLA_V_DIM = 64
MLA_Q_RANK = 384
MLA_KV_RANK = 256
MLA_WIDTH = MLA_HEADS * MLA_V_DIM
ROPE_THETA = 10000.0
HEAD_PAD = 128
MLA_PAD_WIDTH = MLA_HEADS * HEAD_PAD

PEER_HEADS = 8
PEER_N_KEYS = 128
PEER_N_EXPERTS = PEER_N_KEYS * PEER_N_KEYS
PEER_KEY_DIM = 256
PEER_HALF_DIM = PEER_KEY_DIM // 2
PEER_TOPK = 16
PEER_PICKS = PEER_HEADS * PEER_TOPK

_C_SBQ = 0
_C_SBK = SB_WIDTH
_C_SBV = 2 * SB_WIDTH
_C_CQ = 3 * SB_WIDTH
_C_CKV = _C_CQ + MLA_Q_RANK
_C_KPE = _C_CKV + MLA_KV_RANK
IN_PAD_WIDTH = _C_KPE + HEAD_PAD

VMEM_LIMIT = 56 * 1024 * 1024

PREP_TOKENS = 512
ATT_BLOCK = 256
ATT_HEAD_GROUP = 4
MLA_HEAD_GROUP = 8
ROUTE_TOKENS = 256
EXPERT_TOKENS = 256
GATHER_UNROLL = 4
GATHER_AHEAD = 3
GATHER_SLOTS = (GATHER_AHEAD + 1) * GATHER_UNROLL
SLAB_ROWS = 2 * D_MODEL // 128
SLAB_PITCH = SLAB_ROWS + 1
SLOT_ROWS = PEER_PICKS * SLAB_PITCH

SB_DEAD_LOG = -105.0


def _rms(x, g):
    ms = jnp.mean(x * x, axis=-1, keepdims=True)
    return x * lax.rsqrt(ms + EPS) * g


def _prep_kernel(x_ref, pos_ref, an_ref, win_ref, cqn_ref, wuq_ref, ckvn_ref,
                 wuk_ref, wuv_ref, qn_ref, kn_ref, freq_ref,
                 sbq_ref, sbk_ref, sbv_ref, q_ref, k_ref, v_ref):
    h = _rms(x_ref[...], an_ref[...]).astype(jnp.bfloat16)
    proj = jnp.dot(h, win_ref[...], preferred_element_type=jnp.float32)
    sbq_ref[...] = (proj[:, _C_SBQ:_C_SBQ + SB_WIDTH] * (SB_HEAD_DIM ** -0.5)).astype(jnp.bfloat16)
    sbk_ref[...] = proj[:, _C_SBK:_C_SBK + SB_WIDTH].astype(jnp.bfloat16)
    sbv_ref[...] = proj[:, _C_SBV:_C_SBV + SB_WIDTH].astype(jnp.bfloat16)

    cq = _rms(proj[:, _C_CQ:_C_CQ + MLA_Q_RANK], cqn_ref[...]).astype(jnp.bfloat16)
    ckv = _rms(proj[:, _C_CKV:_C_CKV + MLA_KV_RANK], ckvn_ref[...]).astype(jnp.bfloat16)
    kpe = proj[:, _C_KPE:_C_KPE + HEAD_PAD]
    qf = jnp.dot(cq, wuq_ref[...], preferred_element_type=jnp.float32)
    kf = jnp.dot(ckv, wuk_ref[...], preferred_element_type=jnp.float32)
    v_ref[...] = jnp.dot(ckv, wuv_ref[...], preferred_element_type=jnp.float32).astype(jnp.bfloat16)

    ang = pos_ref[...].astype(jnp.float32) * freq_ref[...]
    cos = jnp.cos(ang)
    sin = jnp.sin(ang)
    lane = lax.broadcasted_iota(jnp.int32, ang.shape, 1)
    half = MLA_ROPE_DIM // 2
    lo = (lane >= MLA_NOPE_DIM) & (lane < MLA_NOPE_DIM + half)
    hi = (lane >= MLA_NOPE_DIM + half) & (lane < MLA_QK_DIM)
    sin_lo = jnp.where(lo, -sin, 0.0)
    sin_hi = jnp.where(hi, sin, 0.0)

    def norm_rope(xh, g):
        ms = jnp.sum(xh * xh, axis=-1, keepdims=True) * (1.0 / MLA_QK_DIM)
        xn = xh * lax.rsqrt(ms + EPS) * g
        return (xn * cos + pltpu.roll(xn, HEAD_PAD - half, 1) * sin_lo
                + pltpu.roll(xn, half, 1) * sin_hi)

    for hd in range(MLA_HEADS):
        sl = slice(hd * HEAD_PAD, (hd + 1) * HEAD_PAD)
        q_ref[:, sl] = norm_rope(qf[:, sl], qn_ref[...]).astype(jnp.bfloat16)
        k_ref[:, sl] = norm_rope(kf[:, sl] + kpe, kn_ref[...]).astype(jnp.bfloat16)


def _prep_call(x2d, pos, an, win, cqn, wuq, ckvn, wuk, wuv, qn, kn, freq):
    n = x2d.shape[0]
    t = PREP_TOKENS
    full = lambda a: pl.BlockSpec(a.shape, lambda i: (0,) * a.ndim)
    tok = lambda w: pl.BlockSpec((t, w), lambda i: (i, 0))
    bf = jnp.bfloat16
    return pl.pallas_call(
        _prep_kernel,
        grid=(n // t,),
        in_specs=[tok(D_MODEL), tok(1), full(an), full(win), full(cqn), full(wuq),
                  full(ckvn), full(wuk), full(wuv), full(qn), full(kn), full(freq)],
        out_specs=[tok(SB_WIDTH), tok(SB_WIDTH), tok(SB_WIDTH),
                   tok(MLA_PAD_WIDTH), tok(MLA_PAD_WIDTH), tok(MLA_WIDTH)],
        out_shape=[jax.ShapeDtypeStruct((n, SB_WIDTH), bf)] * 3
        + [jax.ShapeDtypeStruct((n, MLA_PAD_WIDTH), bf)] * 2
        + [jax.ShapeDtypeStruct((n, MLA_WIDTH), bf)],
        compiler_params=pltpu.CompilerParams(
            dimension_semantics=("parallel",), vmem_limit_bytes=VMEM_LIMIT),
        name="prep",
    )(x2d, pos, an, win, cqn, wuq, ckvn, wuk, wuv, qn, kn, freq)


def _sb_kernel(q_ref, k_ref, v_ref, o_ref):
    i = pl.program_id(1)
    tb = ATT_BLOCK
    row = lax.broadcasted_iota(jnp.int32, (tb, tb), 0)
    col = lax.broadcasted_iota(jnp.int32, (tb, tb), 1)
    strict = col < row
    upper = jnp.where(row > col, 1.0, 0.0).astype(jnp.bfloat16)

    for grp in range(SB_HEADS // ATT_HEAD_GROUP):
        heads = range(grp * ATT_HEAD_GROUP, (grp + 1) * ATT_HEAD_GROUP)

        def block(hd, kb, carry, acc, masked):
            sl = slice(hd * SB_HEAD_DIM, (hd + 1) * SB_HEAD_DIM)
            qh = q_ref[0, :, sl]
            start = pl.multiple_of(kb * tb, tb)
            ks = k_ref[0, pl.ds(start, tb), sl]
            vs = v_ref[0, pl.ds(start, tb), sl]
            z = lax.dot_general(qh, ks, (((1,), (1,)), ((), ())),
                                preferred_element_type=jnp.float32)
            lf_all = -(jnp.maximum(z, 0.0) + jnp.log(1.0 + jnp.exp(-jnp.abs(z))))
            lf = jnp.where(strict, lf_all, 0.0) if masked else lf_all
            lf_hi = lf.astype(jnp.bfloat16)
            lf_lo = (lf - lf_hi.astype(jnp.float32)).astype(jnp.bfloat16)
            later = (jnp.dot(lf_hi, upper, preferred_element_type=jnp.float32)
                     + jnp.dot(lf_lo, upper, preferred_element_type=jnp.float32))
            a = jnp.exp(z + lf_all + later + carry)
            if masked:
                a = jnp.where(strict, a, 0.0)
            acc = acc + jnp.dot(a.astype(jnp.bfloat16), vs,
                                preferred_element_type=jnp.float32)
            carry = carry + jnp.sum(lf, axis=-1, keepdims=True)
            return carry, acc

        carry0 = jnp.zeros((tb, 1), jnp.float32)
        acc0 = jnp.zeros((tb, SB_HEAD_DIM), jnp.float32)
        state = [block(hd, i, carry0, acc0, True) for hd in heads]

        def alive(st):
            top = st[0][0]
            for c, _ in st[1:]:
                top = jnp.maximum(top, c)
            return jnp.max(top) > SB_DEAD_LOG

        def cond(s):
            return (s[0] < i) & s[1]

        def body(s):
            j, _, st = s
            st = [block(hd, i - 1 - j, c, a, False) for hd, (c, a) in zip(heads, st)]
            return j + 1, alive(st), st

        _, _, state = lax.while_loop(cond, body, (0, alive(state), state))
        for hd, (_, acc) in zip(heads, state):
            o_ref[0, :, hd * SB_HEAD_DIM:(hd + 1) * SB_HEAD_DIM] = acc


def _sb_call(q, k, v):
    b, s, w = q.shape
    tb = ATT_BLOCK
    return pl.pallas_call(
        _sb_kernel,
        grid=(b, s // tb),
        in_specs=[pl.BlockSpec((1, tb, w), lambda bi, i: (bi, i, 0)),
                  pl.BlockSpec((1, s, w), lambda bi, i: (bi, 0, 0)),
                  pl.BlockSpec((1, s, w), lambda bi, i: (bi, 0, 0))],
        out_specs=pl.BlockSpec((1, tb, w), lambda bi, i: (bi, i, 0)),
        out_shape=jax.ShapeDtypeStruct((b, s, w), jnp.float32),
        compiler_params=pltpu.CompilerParams(
            dimension_semantics=("parallel", "arbitrary"), vmem_limit_bytes=VMEM_LIMIT),
        name="sb_attn",
    )(q, k, v)


def _mla_kernel(q_ref, k_ref, v_ref, o_ref):
    i = pl.program_id(1)
    tb = ATT_BLOCK
    row = lax.broadcasted_iota(jnp.int32, (tb, tb), 0)
    col = lax.broadcasted_iota(jnp.int32, (tb, tb), 1)
    allowed = (col // CHUNK) <= (row // CHUNK)
    scale = MLA_QK_DIM ** -0.5

    def scores(hd, kb):
        qh = q_ref[0, :, hd * HEAD_PAD:(hd + 1) * HEAD_PAD]
        start = pl.multiple_of(kb * tb, tb)
        ks = k_ref[0, pl.ds(start, tb), hd * HEAD_PAD:(hd + 1) * HEAD_PAD]
        vs = v_ref[0, pl.ds(start, tb), hd * MLA_V_DIM:(hd + 1) * MLA_V_DIM]
        sc = lax.dot_general(qh, ks, (((1,), (1,)), ((), ())),
                             preferred_element_type=jnp.float32) * scale
        return sc, vs

    def first(hd):
        sc, vs = scores(hd, i)
        sc = jnp.where(allowed, sc, -jnp.inf)
        m = jnp.max(sc, axis=-1, keepdims=True)
        p = jnp.exp(sc - m)
        l = jnp.sum(p, axis=-1, keepdims=True)
        acc = jnp.dot(p.astype(jnp.bfloat16), vs, preferred_element_type=jnp.float32)
        return m, l, acc

    def update(hd, kb, m, l, acc):
        sc, vs = scores(hd, kb)
        m_new = jnp.maximum(m, jnp.max(sc, axis=-1, keepdims=True))
        alpha = jnp.exp(m - m_new)
        p = jnp.exp(sc - m_new)
        l = alpha * l + jnp.sum(p, axis=-1, keepdims=True)
        acc = alpha * acc + jnp.dot(p.astype(jnp.bfloat16), vs,
                                    preferred_element_type=jnp.float32)
        return m_new, l, acc

    for grp in range(MLA_HEADS // MLA_HEAD_GROUP):
        heads = range(grp * MLA_HEAD_GROUP, (grp + 1) * MLA_HEAD_GROUP)
        state = [first(hd) for hd in heads]

        def body(kb, st):
            return [update(hd, kb, *s) for hd, s in zip(heads, st)]

        state = lax.fori_loop(0, i, body, state)
        for hd, (_, l, acc) in zip(heads, state):
            o_ref[0, :, hd * MLA_V_DIM:(hd + 1) * MLA_V_DIM] = acc / l


def _mla_call(q, k, v):
    b, s, wq = q.shape
    wv = v.shape[-1]
    tb = ATT_BLOCK
    return pl.pallas_call(
        _mla_kernel,
        grid=(b, s // tb),
        in_specs=[pl.BlockSpec((1, tb, wq), lambda bi, i: (bi, i, 0)),
                  pl.BlockSpec((1, s, wq), lambda bi, i: (bi, 0, 0)),
                  pl.BlockSpec((1, s, wv), lambda bi, i: (bi, 0, 0))],
        out_specs=pl.BlockSpec((1, tb, wv), lambda bi, i: (bi, i, 0)),
        out_shape=jax.ShapeDtypeStruct((b, s, wv), jnp.float32),
        compiler_params=pltpu.CompilerParams(
            dimension_semantics=("parallel", "arbitrary"), vmem_limit_bytes=VMEM_LIMIT),
        name="mla_attn",
    )(q, k, v)


def _topk_rows(a, k, ids=None):
    r, t = a.shape
    rows = lax.broadcasted_iota(jnp.int32, (r, t), 0) if ids is None else ids
    krow = lax.broadcasted_iota(jnp.int32, (k, t), 0)
    big = jnp.iinfo(jnp.int32).max

    def body(it, c):
        a, vals, idxs = c
        m = jnp.max(a, axis=0, keepdims=True)
        am = jnp.min(jnp.where(a == m, rows, big), axis=0, keepdims=True)
        a = jnp.where(rows == am, -jnp.inf, a)
        vals = jnp.where(krow == it, m, vals)
        idxs = jnp.where(krow == it, am, idxs)
        return a, vals, idxs

    _, vals, idxs = lax.fori_loop(
        0, k, body, (a, jnp.zeros((k, t), jnp.float32), jnp.zeros((k, t), jnp.int32)))
    return vals, idxs


def _select_rows(sel, table):
    out = jnp.zeros(sel.shape, table.dtype)
    for r in range(table.shape[0]):
        out = jnp.where(sel == r, table[r:r + 1, :], out)
    return out


def _candidate_tiles(k):
    tiles = [("row", 0, s) for s in range(0, k, 8)] + [("col", 0, s) for s in range(0, k, 8)]
    f = 1
    while (f + 1) * (f + 1) <= k:
        tiles += [("row", f, 0), ("col", f, 0)]
        f += 1
    seen, used, ids = set(), [], []
    for kind, fixed, start in tiles:
        tile_ids = []
        for r in range(8):
            pair = (fixed, start + r) if kind == "row" else (start + r, fixed)
            ok = (pair[0] + 1) * (pair[1] + 1) <= k and pair not in seen
            seen.add(pair)
            tile_ids.append(pair[0] * k + pair[1] if ok else -1)
        if max(tile_ids) >= 0:
            used.append((kind, fixed, start))
            ids += tile_ids
    want = sum(1 for a in range(k) for b in range(k) if (a + 1) * (b + 1) <= k)
    assert sum(i >= 0 for i in ids) == want, "candidate tiles must cover the whole staircase"
    return used, ids


_CAND_TILES, _CAND_IDS = _candidate_tiles(PEER_TOPK)


def _route_kernel(x_ref, sb_ref, mla_ref, sbn_ref, mlan_ref, wo_ref, fn_ref, wq_ref,
                  keys_ref, ids_ref, x2_ref, hn_ref, idx_ref, gate_ref, qp_ref):
    bf = jnp.bfloat16
    sbn = _rms(sb_ref[...], sbn_ref[...]).astype(bf)
    mlan = _rms(mla_ref[...], mlan_ref[...]).astype(bf)
    attn = (jnp.dot(sbn, wo_ref[0:SB_WIDTH, :], preferred_element_type=jnp.float32)
            + jnp.dot(mlan, wo_ref[SB_WIDTH:SB_WIDTH + MLA_WIDTH, :],
                      preferred_element_type=jnp.float32))
    x2 = x_ref[...] + attn
    x2_ref[...] = x2
    hn = _rms(x2, fn_ref[...])
    hn_ref[...] = hn
    qp = jnp.dot(hn.astype(bf), wq_ref[...], preferred_element_type=jnp.float32)
    for hc in range(2 * PEER_HEADS):
        qp_ref[hc] = qp[:, hc * PEER_HALF_DIM:(hc + 1) * PEER_HALF_DIM].astype(bf)

    k = PEER_TOPK

    def head(hd, _):
        def sub_scores(c):
            return lax.dot_general(keys_ref[2 * hd + c], qp_ref[2 * hd + c],
                                   (((1,), (1,)), ((), ())),
                                   preferred_element_type=jnp.float32)

        s0, i0 = _topk_rows(sub_scores(0), k)
        s1, i1 = _topk_rows(sub_scores(1), k)
        pieces = []
        for kind, fixed, start in _CAND_TILES:
            if kind == "row":
                pieces.append(s0[fixed:fixed + 1, :] + s1[start:start + 8, :])
            else:
                pieces.append(s0[start:start + 8, :] + s1[fixed:fixed + 1, :])
        ids = ids_ref[...]
        cand = jnp.where(ids >= 0, jnp.concatenate(pieces, axis=0), -jnp.inf)
        best, pos = _topk_rows(cand, k, ids)
        e0 = _select_rows(pos // k, i0)
        e1 = _select_rows(pos % k, i1)
        p = jnp.exp(best - best[0:1, :])
        gate = p / jnp.sum(p, axis=0, keepdims=True)
        rows = pl.ds(pl.multiple_of(hd * k, k), k)
        idx_ref[rows, :] = e0 * PEER_N_KEYS + e1
        for tile in range(gate_ref.shape[0]):
            gate_ref[tile, rows, :] = gate[:, tile * 128:(tile + 1) * 128]
        return 0

    lax.fori_loop(0, PEER_HEADS, head, 0)


def _route_call(x2d, sb, mla, sbn, mlan, wo, fn, wq, keys):
    n = x2d.shape[0]
    t = ROUTE_TOKENS
    full = lambda a: pl.BlockSpec(a.shape, lambda i: (0,) * a.ndim)
    tok = lambda w: pl.BlockSpec((t, w), lambda i: (i, 0))
    tr = pl.BlockSpec((PEER_PICKS, t), lambda i: (0, i))
    ids = jnp.broadcast_to(jnp.asarray(_CAND_IDS, jnp.int32)[:, None], (len(_CAND_IDS), t))
    return pl.pallas_call(
        _route_kernel,
        grid=(n // t,),
        in_specs=[tok(D_MODEL), tok(SB_WIDTH), tok(MLA_WIDTH), full(sbn), full(mlan),
                  full(wo), full(fn), full(wq), full(keys), full(ids)],
        out_specs=[tok(D_MODEL), tok(D_MODEL), tr,
                   pl.BlockSpec((t // 128, PEER_PICKS, 128), lambda i: (i, 0, 0))],
        out_shape=[jax.ShapeDtypeStruct((n, D_MODEL), jnp.float32),
                   jax.ShapeDtypeStruct((n, D_MODEL), jnp.float32),
                   jax.ShapeDtypeStruct((PEER_PICKS, n), jnp.int32),
                   jax.ShapeDtypeStruct((n // 128, PEER_PICKS, 128), jnp.float32)],
        scratch_shapes=[pltpu.VMEM((2 * PEER_HEADS, t, PEER_HALF_DIM), jnp.bfloat16)],
        compiler_params=pltpu.CompilerParams(
            dimension_semantics=("parallel",), vmem_limit_bytes=VMEM_LIMIT),
        name="route",
    )(x2d, sb, mla, sbn, mlan, wo, fn, wq, keys, ids)


def _expert_kernel(idx_hbm, uv_hbm, hn_ref, x2_ref, gate_ref, o_ref,
                   idx_smem, buf_even, buf_odd, idx_sem, row_sem):
    step = pl.program_id(0)
    nsteps = pl.num_programs(0)
    tb = EXPERT_TOKENS
    nu = GATHER_UNROLL
    ahead_b = GATHER_AHEAD
    per_buf = GATHER_SLOTS // 2
    bufs = (buf_even, buf_odd)
    assert ahead_b % 2 == 1 and GATHER_SLOTS == (ahead_b + 1) * nu and (tb // nu) % 4 == 0
    cur = step % 2
    has_next = step + 1 < nsteps

    ring = 2 * tb

    def idx_copy(s, half):
        return pltpu.make_async_copy(
            idx_hbm.at[pl.ds(pl.multiple_of(s * tb, tb), tb), :],
            idx_smem.at[pl.ds(pl.multiple_of(half * tb, tb), tb), :],
            idx_sem.at[half])

    def place(b, j):
        par, half = b
        return par, (half % (per_buf // nu)) * nu + j

    def start_token(ring_row, par, slot):
        base = slot * SLOT_ROWS
        for p in range(PEER_PICKS):
            src = pl.multiple_of(idx_smem[ring_row, p] * SLAB_ROWS, SLAB_ROWS)
            pltpu.make_async_copy(
                uv_hbm.at[pl.ds(src, SLAB_ROWS), :],
                bufs[par].at[pl.ds(base + p * SLAB_PITCH, SLAB_ROWS), :],
                row_sem.at[par * per_buf + slot]).start(priority=p % 2)

    def wait_token(par, slot):
        rows = PEER_PICKS * SLAB_ROWS
        pltpu.make_async_copy(uv_hbm.at[pl.ds(0, rows), :],
                              bufs[par].at[pl.ds(slot * SLOT_ROWS, rows), :],
                              row_sem.at[par * per_buf + slot]).wait()

    @pl.when(step == 0)
    def _():
        idx_copy(0, 0).start()
        idx_copy(0, 0).wait()
        for b in range(ahead_b):
            for j in range(nu):
                start_token(b * nu + j, *place((b % 2, b // 2), j))

    @pl.when(has_next)
    def _():
        idx_copy(step + 1, 1 - cur).start()

    lane = lax.broadcasted_iota(jnp.int32, (PEER_PICKS, 128), 1)
    groups = PEER_PICKS // 8
    tiles = D_MODEL // 128

    def compute_token(t, par, slot):
        buf = bufs[par]
        base = slot * SLOT_ROWS
        xrow = hn_ref[pl.ds(t, 1), :]
        pres = []
        for g in range(groups):
            acc = None
            for c in range(tiles):
                u = buf[pl.ds(base + g * 8 * SLAB_PITCH + c, 8, stride=SLAB_PITCH), :]
                term = u * xrow[:, c * 128:(c + 1) * 128]
                acc = term if acc is None else acc + term
            pres.append(jnp.sum(acc, axis=1, keepdims=True))
        pre = jnp.concatenate(pres, axis=0)
        gate = jnp.sum(jnp.where(lane == t % 128, gate_ref[t // 128], 0.0),
                       axis=1, keepdims=True)
        coef = gate * (0.5 * pre * (1.0 + lax.erf(pre * (2.0 ** -0.5))))
        outs = []
        for c in range(tiles):
            acc = None
            for g in range(groups):
                v = buf[pl.ds(base + g * 8 * SLAB_PITCH + tiles + c, 8, stride=SLAB_PITCH), :]
                term = coef[g * 8:(g + 1) * 8, :] * v
                acc = term if acc is None else acc + term
            outs.append(jnp.sum(acc, axis=0, keepdims=True))
        out = jnp.concatenate(outs, axis=1)
        o_ref[pl.ds(t, 1), :] = x2_ref[pl.ds(t, 1), :] + out

    def bunch(par, half, guard_refill):
        first = (2 * half + par) * nu
        for j in range(nu):
            wait_token(*place((par, half), j))
        for j in range(nu):
            compute_token(first + j, *place((par, half), j))

        def refill():
            later = (par + ahead_b) // 2 + half
            for j in range(nu):
                start_token((cur * tb + first + ahead_b * nu + j) % ring,
                            *place((1 - par, later), j))

        if guard_refill:
            pl.when(has_next | (first + ahead_b * nu < tb))(refill)
        else:
            refill()

    def pair(i, _):
        bunch(0, i, False)
        bunch(1, i, False)
        return 0

    pairs = tb // (2 * nu)
    tail_pairs = (ahead_b + 1) // 2
    lax.fori_loop(0, pairs - tail_pairs, pair, 0)

    @pl.when(has_next)
    def _():
        idx_copy(step + 1, 1 - cur).wait()

    def tail_pair(i, _):
        bunch(0, i, True)
        bunch(1, i, True)
        return 0

    lax.fori_loop(pairs - tail_pairs, pairs, tail_pair, 0)


def _expert_call(idx, uv, hn, x2, gate_t):
    n = hn.shape[0]
    tb = EXPERT_TOKENS
    tok = pl.BlockSpec((tb, D_MODEL), lambda i: (i, 0))
    return pl.pallas_call(
        _expert_kernel,
        grid=(n // tb,),
        in_specs=[pl.BlockSpec(memory_space=pl.ANY),
                  pl.BlockSpec(memory_space=pl.ANY),
                  tok, tok,
                  pl.BlockSpec((tb // 128, PEER_PICKS, 128), lambda i: (i, 0, 0))],
        out_specs=tok,
        out_shape=jax.ShapeDtypeStruct((n, D_MODEL), jnp.float32),
        scratch_shapes=[pltpu.SMEM((2 * tb, PEER_PICKS), jnp.int32),
                        pltpu.VMEM((GATHER_SLOTS // 2 * SLOT_ROWS, 128), jnp.float32),
                        pltpu.VMEM((GATHER_SLOTS // 2 * SLOT_ROWS, 128), jnp.float32),
                        pltpu.SemaphoreType.DMA((2,)),
                        pltpu.SemaphoreType.DMA((GATHER_SLOTS,))],
        compiler_params=pltpu.CompilerParams(
            dimension_semantics=("arbitrary",), vmem_limit_bytes=VMEM_LIMIT),
        name="experts",
    )(idx, uv, hn, x2, gate_t)


def _pad_heads(w, head_dim):
    k = w.shape[0]
    w = w.reshape(k, -1, head_dim)
    w = jnp.pad(w, ((0, 0), (0, 0), (0, HEAD_PAD - head_dim)))
    return w.reshape(k, -1)


def _layer(x2d, pos, b, s, an, w_in, cqn, w_uq, ckvn, w_ukv, qn, kn, sbn, mlan, w_o, fn,
           w_pq, sub_keys, pu, pv):
    bf = jnp.bfloat16
    row = lambda a: a.reshape(1, -1)
    kpe_cols = jnp.pad(w_in[:, _C_KPE:_C_KPE + MLA_ROPE_DIM],
                       ((0, 0), (MLA_NOPE_DIM, HEAD_PAD - MLA_QK_DIM)))
    win = jnp.concatenate([w_in[:, :_C_KPE], kpe_cols], axis=1).astype(bf)
    wuq = _pad_heads(w_uq, MLA_QK_DIM).astype(bf)
    w_ukv3 = w_ukv.reshape(MLA_KV_RANK, MLA_HEADS, MLA_NOPE_DIM + MLA_V_DIM)
    wuk = _pad_heads(w_ukv3[:, :, :MLA_NOPE_DIM].reshape(MLA_KV_RANK, -1), MLA_NOPE_DIM).astype(bf)
    wuv = w_ukv3[:, :, MLA_NOPE_DIM:].reshape(MLA_KV_RANK, -1).astype(bf)
    pad_gain = lambda g: jnp.pad(g, (0, HEAD_PAD - MLA_QK_DIM)).reshape(1, HEAD_PAD)
    half = MLA_ROPE_DIM // 2
    inv_freq = 1.0 / (ROPE_THETA ** (jnp.arange(half, dtype=jnp.float32) * (2.0 / MLA_ROPE_DIM)))
    freq = jnp.concatenate([jnp.zeros((MLA_NOPE_DIM,), jnp.float32), inv_freq, inv_freq,
                            jnp.zeros((HEAD_PAD - MLA_QK_DIM,), jnp.float32)]).reshape(1, HEAD_PAD)

    sbq, sbk, sbv, q, k, v = _prep_call(
        x2d, pos, row(an), win, row(cqn), wuq, row(ckvn), wuk, wuv,
        pad_gain(qn), pad_gain(kn), freq)

    r3 = lambda a: a.reshape(b, s, a.shape[-1])
    sb = _sb_call(r3(sbq), r3(sbk), r3(sbv)).reshape(b * s, SB_WIDTH)
    mla = _mla_call(r3(q), r3(k), r3(v)).reshape(b * s, MLA_WIDTH)

    keys = sub_keys.transpose(1, 0, 2, 3).reshape(2 * PEER_HEADS, PEER_N_KEYS, PEER_HALF_DIM).astype(bf)
    x2, hn, idx_t, gate_t = _route_call(
        x2d, sb, mla, row(sbn), row(mlan), w_o.astype(bf), row(fn), w_pq.astype(bf), keys)

    slab = lambda w: w.reshape(PEER_N_EXPERTS, D_MODEL // 128, 128)
    uv = jnp.concatenate([slab(pu), slab(pv)], axis=1).reshape(PEER_N_EXPERTS * SLAB_ROWS, 128)
    return _expert_call(idx_t.T, uv, hn, x2, gate_t)


def kernel(x, positions, attn_norm, w_in, cq_norm, w_uq, ckv_norm, w_ukv, q_norm, k_norm,
           sb_out_norm, mla_out_norm, w_o, ffn_norm, peer_w_q, peer_sub_keys, peer_u, peer_v):
    b, s, d = x.shape
    x2d = x.reshape(b * s, d)
    pos = positions.reshape(b * s, 1)
    for l in range(attn_norm.shape[0]):
        x2d = _layer(x2d, pos, b, s, attn_norm[l], w_in[l], cq_norm[l], w_uq[l], ckv_norm[l],
                     w_ukv[l], q_norm[l], k_norm[l], sb_out_norm[l], mla_out_norm[l], w_o[l],
                     ffn_norm[l], peer_w_q[l], peer_sub_keys[l], peer_u[l], peer_v[l])
    return x2d.reshape(b, s, d)
```

```python
import functools
import math

import jax
import jax.numpy as jnp
from jax import lax
from jax.experimental import pallas as pl
from jax.experimental.pallas import tpu as pltpu

D_MODEL = 1024
CHUNK = 64
EPS = 1e-6

SB_HEADS = 8
SB_HEAD_DIM = 64
SB_WIDTH = SB_HEADS * SB_HEAD_DIM

MLA_HEADS = 8
MLA_NOPE_DIM = 64
MLA_ROPE_DIM = 32
MLA_QK_DIM = MLA_NOPE_DIM + MLA_ROPE_DIM
MLA_V_DIM = 64
MLA_Q_RANK = 384
MLA_KV_RANK = 256
MLA_WIDTH = MLA_HEADS * MLA_V_DIM
ROPE_THETA = 10000.0
HEAD_PAD = 128
MLA_PAD_WIDTH = MLA_HEADS * HEAD_PAD

PEER_HEADS = 8
PEER_N_KEYS = 128
PEER_N_EXPERTS = PEER_N_KEYS * PEER_N_KEYS
PEER_KEY_DIM = 256
PEER_HALF_DIM = PEER_KEY_DIM // 2
PEER_TOPK = 16
PEER_PICKS = PEER_HEADS * PEER_TOPK

_C_SBQ = 0
_C_SBK = SB_WIDTH
_C_SBV = 2 * SB_WIDTH
_C_CQ = 3 * SB_WIDTH
_C_CKV = _C_CQ + MLA_Q_RANK
_C_KPE = _C_CKV + MLA_KV_RANK
IN_PAD_WIDTH = _C_KPE + HEAD_PAD

VMEM_LIMIT = 56 * 1024 * 1024

PREP_TOKENS = 512
ATT_BLOCK = 256
ATT_HEAD_GROUP = 8
MLA_HEAD_GROUP = 8
ROUTE_TOKENS = 256
EXPERT_TOKENS = 256
GATHER_UNROLL = 4
GATHER_AHEAD = 3
GATHER_SLOTS = (GATHER_AHEAD + 1) * GATHER_UNROLL
SLAB_ROWS = 2 * D_MODEL // 128
SLAB_PITCH = SLAB_ROWS + 1
SLOT_ROWS = PEER_PICKS * SLAB_PITCH

SB_DEAD_LOG = -105.0


def _rms(x, g):
    ms = jnp.mean(x * x, axis=-1, keepdims=True)
    return x * lax.rsqrt(ms + EPS) * g


def _prep_kernel(x_ref, pos_ref, an_ref, win_ref, cqn_ref, wuq_ref, ckvn_ref,
                 wuk_ref, wuv_ref, qn_ref, kn_ref, freq_ref,
                 sbq_ref, sbk_ref, sbv_ref, q_ref, k_ref, vt_ref):
    h = _rms(x_ref[...], an_ref[...]).astype(jnp.bfloat16)
    proj = jnp.dot(h, win_ref[...], preferred_element_type=jnp.float32)
    sbq_ref[...] = (proj[:, _C_SBQ:_C_SBQ + SB_WIDTH] * (SB_HEAD_DIM ** -0.5)).astype(jnp.bfloat16)
    sbk_ref[...] = proj[:, _C_SBK:_C_SBK + SB_WIDTH].astype(jnp.bfloat16)
    sbv_ref[...] = proj[:, _C_SBV:_C_SBV + SB_WIDTH].astype(jnp.bfloat16)

    cq = _rms(proj[:, _C_CQ:_C_CQ + MLA_Q_RANK], cqn_ref[...]).astype(jnp.bfloat16)
    ckv = _rms(proj[:, _C_CKV:_C_CKV + MLA_KV_RANK], ckvn_ref[...]).astype(jnp.bfloat16)
    kpe = proj[:, _C_KPE:_C_KPE + HEAD_PAD]
    qf = jnp.dot(cq, wuq_ref[...], preferred_element_type=jnp.float32)
    kf = jnp.dot(ckv, wuk_ref[...], preferred_element_type=jnp.float32)
    vf = jnp.dot(ckv, wuv_ref[...], preferred_element_type=jnp.float32)
    for blk in range(vt_ref.shape[0]):
        vt_ref[blk] = vf[blk * ATT_BLOCK:(blk + 1) * ATT_BLOCK, :].T.astype(jnp.bfloat16)

    ang = pos_ref[...].astype(jnp.float32) * freq_ref[...]
    cos = jnp.cos(ang)
    sin = jnp.sin(ang)
    lane = lax.broadcasted_iota(jnp.int32, ang.shape, 1)
    half = MLA_ROPE_DIM // 2
    lo = (lane >= MLA_NOPE_DIM) & (lane < MLA_NOPE_DIM + half)
    hi = (lane >= MLA_NOPE_DIM + half) & (lane < MLA_QK_DIM)
    sin_lo = jnp.where(lo, -sin, 0.0)
    sin_hi = jnp.where(hi, sin, 0.0)

    def norm_rope(xh, g):
        ms = jnp.sum(xh * xh, axis=-1, keepdims=True) * (1.0 / MLA_QK_DIM)
        xn = xh * lax.rsqrt(ms + EPS) * g
        return (xn * cos + pltpu.roll(xn, HEAD_PAD - half, 1) * sin_lo
                + pltpu.roll(xn, half, 1) * sin_hi)

    for hd in range(MLA_HEADS):
        sl = slice(hd * HEAD_PAD, (hd + 1) * HEAD_PAD)
        q_ref[:, sl] = norm_rope(qf[:, sl], qn_ref[...]).astype(jnp.bfloat16)
        k_ref[:, sl] = norm_rope(kf[:, sl] + kpe, kn_ref[...]).astype(jnp.bfloat16)


def _prep_call(x2d, pos, an, win, cqn, wuq, ckvn, wuk, wuv, qn, kn, freq):
    n = x2d.shape[0]
    t = PREP_TOKENS
    full = lambda a: pl.BlockSpec(a.shape, lambda i: (0,) * a.ndim)
    tok = lambda w: pl.BlockSpec((t, w), lambda i: (i, 0))
    bf = jnp.bfloat16
    return pl.pallas_call(
        _prep_kernel,
        grid=(n // t,),
        in_specs=[tok(D_MODEL), tok(1), full(an), full(win), full(cqn), full(wuq),
                  full(ckvn), full(wuk), full(wuv), full(qn), full(kn), full(freq)],
        out_specs=[tok(SB_WIDTH), tok(SB_WIDTH), tok(SB_WIDTH),
                   tok(MLA_PAD_WIDTH), tok(MLA_PAD_WIDTH),
                   pl.BlockSpec((t // ATT_BLOCK, MLA_WIDTH, ATT_BLOCK), lambda i: (i, 0, 0))],
        out_shape=[jax.ShapeDtypeStruct((n, SB_WIDTH), bf)] * 3
        + [jax.ShapeDtypeStruct((n, MLA_PAD_WIDTH), bf)] * 2
        + [jax.ShapeDtypeStruct((n // ATT_BLOCK, MLA_WIDTH, ATT_BLOCK), bf)],
        compiler_params=pltpu.CompilerParams(
            dimension_semantics=("parallel",), vmem_limit_bytes=VMEM_LIMIT),
        name="prep",
    )(x2d, pos, an, win, cqn, wuq, ckvn, wuk, wuv, qn, kn, freq)


def _sb_kernel(q_ref, k_ref, v_ref, o_ref):
    i = pl.program_id(1)
    tb = ATT_BLOCK
    row = lax.broadcasted_iota(jnp.int32, (tb, tb), 0)
    col = lax.broadcasted_iota(jnp.int32, (tb, tb), 1)
    strict = col < row
    upper = jnp.where(row > col, 1.0, 0.0).astype(jnp.bfloat16)

    for grp in range(SB_HEADS // ATT_HEAD_GROUP):
        heads = range(grp * ATT_HEAD_GROUP, (grp + 1) * ATT_HEAD_GROUP)

        def walk(kb, st, masked):
            start = pl.multiple_of(kb * tb, tb)
            heads_sl = [slice(hd * SB_HEAD_DIM, (hd + 1) * SB_HEAD_DIM) for hd in heads]
            zs = [lax.dot_general(q_ref[0, :, sl], k_ref[0, pl.ds(start, tb), sl],
                                  (((1,), (1,)), ((), ())), preferred_element_type=jnp.float32)
                  for sl in heads_sl]
            mids = []
            for z in zs:
                lf_all = -(jnp.maximum(z, 0.0) + jnp.log(1.0 + jnp.exp(-jnp.abs(z))))
                lf = jnp.where(strict, lf_all, 0.0) if masked else lf_all
                lf_hi = lf.astype(jnp.bfloat16)
                lf_lo = (lf - lf_hi.astype(jnp.float32)).astype(jnp.bfloat16)
                mids.append((z + lf_all, lf_hi, lf_lo, jnp.sum(lf, axis=-1, keepdims=True)))
            laters = [jnp.dot(hi, upper, preferred_element_type=jnp.float32)
                      + jnp.dot(lo, upper, preferred_element_type=jnp.float32)
                      for _, hi, lo, _ in mids]
            weights = []
            for (ls, _, _, _), later, (carry, _) in zip(mids, laters, st):
                a = jnp.exp(ls + later + carry)
                if masked:
                    a = jnp.where(strict, a, 0.0)
                weights.append(a.astype(jnp.bfloat16))
            return [(carry + rs,
                     acc + jnp.dot(a, v_ref[0, pl.ds(start, tb), sl],
                                   preferred_element_type=jnp.float32))
                    for a, sl, (_, _, _, rs), (carry, acc) in zip(weights, heads_sl, mids, st)]

        carry0 = jnp.zeros((tb, 1), jnp.float32)
        acc0 = jnp.zeros((tb, SB_HEAD_DIM), jnp.float32)
        state = walk(i, [(carry0, acc0)] * len(heads), True)

        def alive(st):
            top = st[0][0]
            for c, _ in st[1:]:
                top = jnp.maximum(top, c)
            return jnp.max(top) > SB_DEAD_LOG

        def cond(s):
            return (s[0] < i) & s[1]

        def body(s):
            j, _, st = s
            st = walk(i - 1 - j, st, False)
            return j + 1, alive(st), st

        _, _, state = lax.while_loop(cond, body, (0, alive(state), state))
        for hd, (_, acc) in zip(heads, state):
            o_ref[0, :, hd * SB_HEAD_DIM:(hd + 1) * SB_HEAD_DIM] = acc


def _sb_call(q, k, v):
    b, s, w = q.shape
    tb = ATT_BLOCK
    return pl.pallas_call(
        _sb_kernel,
        grid=(b, s // tb),
        in_specs=[pl.BlockSpec((1, tb, w), lambda bi, i: (bi, i, 0)),
                  pl.BlockSpec((1, s, w), lambda bi, i: (bi, 0, 0)),
                  pl.BlockSpec((1, s, w), lambda bi, i: (bi, 0, 0))],
        out_specs=pl.BlockSpec((1, tb, w), lambda bi, i: (bi, i, 0)),
        out_shape=jax.ShapeDtypeStruct((b, s, w), jnp.float32),
        compiler_params=pltpu.CompilerParams(
            dimension_semantics=("parallel", "arbitrary"), vmem_limit_bytes=VMEM_LIMIT),
        name="sb_attn",
    )(q, k, v)


def _mla_kernel(q_ref, k_ref, vt_ref, o_ref):
    i = pl.program_id(1)
    tb = ATT_BLOCK
    key = lax.broadcasted_iota(jnp.int32, (tb, tb), 0)
    qry = lax.broadcasted_iota(jnp.int32, (tb, tb), 1)
    allowed = (key // CHUNK) <= (qry // CHUNK)
    scale = MLA_QK_DIM ** -0.5

    def scores(hd, kb):
        qh = q_ref[0, :, hd * HEAD_PAD:(hd + 1) * HEAD_PAD]
        start = pl.multiple_of(kb * tb, tb)
        ks = k_ref[0, pl.ds(start, tb), hd * HEAD_PAD:(hd + 1) * HEAD_PAD]
        vt = vt_ref[0, kb, hd * MLA_V_DIM:(hd + 1) * MLA_V_DIM, :]
        st = lax.dot_general(ks, qh, (((1,), (1,)), ((), ())),
                             preferred_element_type=jnp.float32) * scale
        return st, vt

    outs = []
    for grp in range(MLA_HEADS // MLA_HEAD_GROUP):
        heads = range(grp * MLA_HEAD_GROUP, (grp + 1) * MLA_HEAD_GROUP)

        def walk(kb, st, masked):
            tiles = [scores(hd, kb) for hd in heads]
            mids = []
            for (s_t, vt), (m, l, acc) in zip(tiles, st):
                if masked:
                    s_t = jnp.where(allowed, s_t, -jnp.inf)
                m_new = jnp.maximum(m, jnp.max(s_t, axis=0, keepdims=True))
                alpha = jnp.exp(m - m_new)
                p = jnp.exp(s_t - m_new)
                l = alpha * l + jnp.sum(p, axis=0, keepdims=True)
                mids.append((m_new, l, alpha, p.astype(jnp.bfloat16), vt, acc))
            return [(m, l, alpha * acc + jnp.dot(vt, p, preferred_element_type=jnp.float32))
                    for m, l, alpha, p, vt, acc in mids]

        empty = (jnp.full((1, tb), -jnp.inf, jnp.float32), jnp.zeros((1, tb), jnp.float32),
                 jnp.zeros((MLA_V_DIM, tb), jnp.float32))
        state = walk(i, [empty] * len(heads), True)
        state = lax.fori_loop(0, i, lambda kb, st: walk(kb, st, False), state)
        outs += [acc / l for _, l, acc in state]
    o_ref[0] = jnp.concatenate(outs, axis=0).T


def _mla_call(q, k, vt):
    b, s, wq = q.shape
    _, nblk, wv, tb = vt.shape
    return pl.pallas_call(
        _mla_kernel,
        grid=(b, s // tb),
        in_specs=[pl.BlockSpec((1, tb, wq), lambda bi, i: (bi, i, 0)),
                  pl.BlockSpec((1, s, wq), lambda bi, i: (bi, 0, 0)),
                  pl.BlockSpec((1, nblk, wv, tb), lambda bi, i: (bi, 0, 0, 0))],
        out_specs=pl.BlockSpec((1, tb, wv), lambda bi, i: (bi, i, 0)),
        out_shape=jax.ShapeDtypeStruct((b, s, wv), jnp.float32),
        compiler_params=pltpu.CompilerParams(
            dimension_semantics=("parallel", "arbitrary"), vmem_limit_bytes=VMEM_LIMIT),
        name="mla_attn",
    )(q, k, vt)


def _topk_rows(a, k, ids=None):
    r, t = a.shape
    rows = lax.broadcasted_iota(jnp.int32, (r, t), 0) if ids is None else ids
    krow = lax.broadcasted_iota(jnp.int32, (k, t), 0)
    big = jnp.iinfo(jnp.int32).max

    def body(it, c):
        a, vals, idxs = c
        m = jnp.max(a, axis=0, keepdims=True)
        am = jnp.min(jnp.where(a == m, rows, big), axis=0, keepdims=True)
        a = jnp.where(rows == am, -jnp.inf, a)
        vals = jnp.where(krow == it, m, vals)
        idxs = jnp.where(krow == it, am, idxs)
        return a, vals, idxs

    _, vals, idxs = lax.fori_loop(
        0, k, body, (a, jnp.zeros((k, t), jnp.float32), jnp.zeros((k, t), jnp.int32)))
    return vals, idxs


def _select_rows(sel, table):
    out = jnp.zeros(sel.shape, table.dtype)
    for r in range(table.shape[0]):
        out = jnp.where(sel == r, table[r:r + 1, :], out)
    return out


def _candidate_tiles(k):
    tiles = [("row", 0, s) for s in range(0, k, 8)] + [("col", 0, s) for s in range(0, k, 8)]
    f = 1
    while (f + 1) * (f + 1) <= k:
        tiles += [("row", f, 0), ("col", f, 0)]
        f += 1
    seen, used, ids = set(), [], []
    for kind, fixed, start in tiles:
        tile_ids = []
        for r in range(8):
            pair = (fixed, start + r) if kind == "row" else (start + r, fixed)
            ok = (pair[0] + 1) * (pair[1] + 1) <= k and pair not in seen
            seen.add(pair)
            tile_ids.append(pair[0] * k + pair[1] if ok else -1)
        if max(tile_ids) >= 0:
            used.append((kind, fixed, start))
            ids += tile_ids
    want = sum(1 for a in range(k) for b in range(k) if (a + 1) * (b + 1) <= k)
    assert sum(i >= 0 for i in ids) == want, "candidate tiles must cover the whole staircase"
    return used, ids


_CAND_TILES, _CAND_IDS = _candidate_tiles(PEER_TOPK)


def _route_kernel(x_ref, sb_ref, mla_ref, sbn_ref, mlan_ref, wo_ref, fn_ref, wq_ref,
                  keys_ref, ids_ref, x2_ref, hn_ref, idx_ref, gate_ref, qp_ref):
    bf = jnp.bfloat16
    sbn = _rms(sb_ref[...], sbn_ref[...]).astype(bf)
    mlan = _rms(mla_ref[...], mlan_ref[...]).astype(bf)
    attn = (jnp.dot(sbn, wo_ref[0:SB_WIDTH, :], preferred_element_type=jnp.float32)
            + jnp.dot(mlan, wo_ref[SB_WIDTH:SB_WIDTH + MLA_WIDTH, :],
                      preferred_element_type=jnp.float32))
    x2 = x_ref[...] + attn
    x2_ref[...] = x2
    hn = _rms(x2, fn_ref[...])
    hn_ref[...] = hn
    qp = jnp.dot(hn.astype(bf), wq_ref[...], preferred_element_type=jnp.float32)
    for hc in range(2 * PEER_HEADS):
        qp_ref[hc] = qp[:, hc * PEER_HALF_DIM:(hc + 1) * PEER_HALF_DIM].astype(bf)

    k = PEER_TOPK

    def head(hd, _):
        def sub_scores(c):
            return lax.dot_general(keys_ref[2 * hd + c], qp_ref[2 * hd + c],
                                   (((1,), (1,)), ((), ())),
                                   preferred_element_type=jnp.float32)

        s0, i0 = _topk_rows(sub_scores(0), k)
        s1, i1 = _topk_rows(sub_scores(1), k)
        pieces = []
        for kind, fixed, start in _CAND_TILES:
            if kind == "row":
                pieces.append(s0[fixed:fixed + 1, :] + s1[start:start + 8, :])
            else:
                pieces.append(s0[start:start + 8, :] + s1[fixed:fixed + 1, :])
        ids = ids_ref[...]
        cand = jnp.where(ids >= 0, jnp.concatenate(pieces, axis=0), -jnp.inf)
        best, pos = _topk_rows(cand, k, ids)
        e0 = _select_rows(pos // k, i0)
        e1 = _select_rows(pos % k, i1)
        p = jnp.exp(best - best[0:1, :])
        gate = p / jnp.sum(p, axis=0, keepdims=True)
        rows = pl.ds(pl.multiple_of(hd * k, k), k)
        idx_ref[rows, :] = e0 * PEER_N_KEYS + e1
        for tile in range(gate_ref.shape[0]):
            gate_ref[tile, rows, :] = gate[:, tile * 128:(tile + 1) * 128]
        return 0

    lax.fori_loop(0, PEER_HEADS, head, 0)


def _route_call(x2d, sb, mla, sbn, mlan, wo, fn, wq, keys):
    n = x2d.shape[0]
    t = ROUTE_TOKENS
    full = lambda a: pl.BlockSpec(a.shape, lambda i: (0,) * a.ndim)
    tok = lambda w: pl.BlockSpec((t, w), lambda i: (i, 0))
    tr = pl.BlockSpec((PEER_PICKS, t), lambda i: (0, i))
    ids = jnp.broadcast_to(jnp.asarray(_CAND_IDS, jnp.int32)[:, None], (len(_CAND_IDS), t))
    return pl.pallas_call(
        _route_kernel,
        grid=(n // t,),
        in_specs=[tok(D_MODEL), tok(SB_WIDTH), tok(MLA_WIDTH), full(sbn), full(mlan),
                  full(wo), full(fn), full(wq), full(keys), full(ids)],
        out_specs=[tok(D_MODEL), tok(D_MODEL), tr,
                   pl.BlockSpec((t // 128, PEER_PICKS, 128), lambda i: (i, 0, 0))],
        out_shape=[jax.ShapeDtypeStruct((n, D_MODEL), jnp.float32),
                   jax.ShapeDtypeStruct((n, D_MODEL), jnp.float32),
                   jax.ShapeDtypeStruct((PEER_PICKS, n), jnp.int32),
                   jax.ShapeDtypeStruct((n // 128, PEER_PICKS, 128), jnp.float32)],
        scratch_shapes=[pltpu.VMEM((2 * PEER_HEADS, t, PEER_HALF_DIM), jnp.bfloat16)],
        compiler_params=pltpu.CompilerParams(
            dimension_semantics=("parallel",), vmem_limit_bytes=VMEM_LIMIT),
        name="route",
    )(x2d, sb, mla, sbn, mlan, wo, fn, wq, keys, ids)


def _expert_kernel(idx_hbm, uv_hbm, hn_ref, x2_ref, gate_ref, o_ref,
                   idx_smem, buf_even, buf_odd, idx_sem, row_sem):
    step = pl.program_id(0)
    nsteps = pl.num_programs(0)
    tb = EXPERT_TOKENS
    nu = GATHER_UNROLL
    ahead_b = GATHER_AHEAD
    per_buf = GATHER_SLOTS // 2
    bufs = (buf_even, buf_odd)
    assert ahead_b % 2 == 1 and GATHER_SLOTS == (ahead_b + 1) * nu and (tb // nu) % 4 == 0
    cur = step % 2
    has_next = step + 1 < nsteps

    ring = 2 * tb

    def idx_copy(s, half):
        return pltpu.make_async_copy(
            idx_hbm.at[pl.ds(pl.multiple_of(s * tb, tb), tb), :],
            idx_smem.at[pl.ds(pl.multiple_of(half * tb, tb), tb), :],
            idx_sem.at[half])

    def place(b, j):
        par, half = b
        return par, (half % (per_buf // nu)) * nu + j

    def start_token(ring_row, par, slot):
        base = slot * SLOT_ROWS
        for p in range(PEER_PICKS):
            src = pl.multiple_of(idx_smem[ring_row, p] * SLAB_ROWS, SLAB_ROWS)
            pltpu.make_async_copy(
                uv_hbm.at[pl.ds(src, SLAB_ROWS), :],
                bufs[par].at[pl.ds(base + p * SLAB_PITCH, SLAB_ROWS), :],
                row_sem.at[par * per_buf + slot]).start(priority=p % 2)

    def wait_token(par, slot):
        rows = PEER_PICKS * SLAB_ROWS
        pltpu.make_async_copy(uv_hbm.at[pl.ds(0, rows), :],
                              bufs[par].at[pl.ds(slot * SLOT_ROWS, rows), :],
                              row_sem.at[par * per_buf + slot]).wait()

    @pl.when(step == 0)
    def _():
        idx_copy(0, 0).start()
        idx_copy(0, 0).wait()
        for b in range(ahead_b):
            for j in range(nu):
                start_token(b * nu + j, *place((b % 2, b // 2), j))

    @pl.when(has_next)
    def _():
        idx_copy(step + 1, 1 - cur).start()

    lane = lax.broadcasted_iota(jnp.int32, (PEER_PICKS, 128), 1)
    groups = PEER_PICKS // 8
    tiles = D_MODEL // 128

    def compute_token(t, par, slot):
        buf = bufs[par]
        base = slot * SLOT_ROWS
        xrow = hn_ref[pl.ds(t, 1), :]
        pres = []
        for g in range(groups):
            acc = None
            for c in range(tiles):
                u = buf[pl.ds(base + g * 8 * SLAB_PITCH + c, 8, stride=SLAB_PITCH), :]
                term = u * xrow[:, c * 128:(c + 1) * 128]
                acc = term if acc is None else acc + term
            pres.append(jnp.sum(acc, axis=1, keepdims=True))
        pre = jnp.concatenate(pres, axis=0)
        gate = jnp.sum(jnp.where(lane == t % 128, gate_ref[t // 128], 0.0),
                       axis=1, keepdims=True)
        coef = gate * (0.5 * pre * (1.0 + lax.erf(pre * (2.0 ** -0.5))))
        outs = []
        for c in range(tiles):
            acc = None
            for g in range(groups):
                v = buf[pl.ds(base + g * 8 * SLAB_PITCH + tiles + c, 8, stride=SLAB_PITCH), :]
                term = coef[g * 8:(g + 1) * 8, :] * v
                acc = term if acc is None else acc + term
            outs.append(jnp.sum(acc, axis=0, keepdims=True))
        out = jnp.concatenate(outs, axis=1)
        o_ref[pl.ds(t, 1), :] = x2_ref[pl.ds(t, 1), :] + out

    def bunch(par, half, guard_refill):
        first = (2 * half + par) * nu
        for j in range(nu):
            wait_token(*place((par, half), j))
        for j in range(nu):
            compute_token(first + j, *place((par, half), j))

        def refill():
            later = (par + ahead_b) // 2 + half
            for j in range(nu):
                start_token((cur * tb + first + ahead_b * nu + j) % ring,
                            *place((1 - par, later), j))

        if guard_refill:
            pl.when(has_next | (first + ahead_b * nu < tb))(refill)
        else:
            refill()

    def pair(i, _):
        bunch(0, i, False)
        bunch(1, i, False)
        return 0

    pairs = tb // (2 * nu)
    tail_pairs = (ahead_b + 1) // 2
    lax.fori_loop(0, pairs - tail_pairs, pair, 0)

    @pl.when(has_next)
    def _():
        idx_copy(step + 1, 1 - cur).wait()

    def tail_pair(i, _):
        bunch(0, i, True)
        bunch(1, i, True)
        return 0

    lax.fori_loop(pairs - tail_pairs, pairs, tail_pair, 0)


def _expert_call(idx, uv, hn, x2, gate_t):
    n = hn.shape[0]
    tb = EXPERT_TOKENS
    tok = pl.BlockSpec((tb, D_MODEL), lambda i: (i, 0))
    return pl.pallas_call(
        _expert_kernel,
        grid=(n // tb,),
        in_specs=[pl.BlockSpec(memory_space=pl.ANY),
                  pl.BlockSpec(memory_space=pl.ANY),
                  tok, tok,
                  pl.BlockSpec((tb // 128, PEER_PICKS, 128), lambda i: (i, 0, 0))],
        out_specs=tok,
        out_shape=jax.ShapeDtypeStruct((n, D_MODEL), jnp.float32),
        scratch_shapes=[pltpu.SMEM((2 * tb, PEER_PICKS), jnp.int32),
                        pltpu.VMEM((GATHER_SLOTS // 2 * SLOT_ROWS, 128), jnp.float32),
                        pltpu.VMEM((GATHER_SLOTS // 2 * SLOT_ROWS, 128), jnp.float32),
                        pltpu.SemaphoreType.DMA((2,)),
                        pltpu.SemaphoreType.DMA((GATHER_SLOTS,))],
        compiler_params=pltpu.CompilerParams(
            dimension_semantics=("arbitrary",), vmem_limit_bytes=VMEM_LIMIT),
        name="experts",
    )(idx, uv, hn, x2, gate_t)


def _pad_heads(w, head_dim):
    k = w.shape[0]
    w = w.reshape(k, -1, head_dim)
    w = jnp.pad(w, ((0, 0), (0, 0), (0, HEAD_PAD - head_dim)))
    return w.reshape(k, -1)


def _layer(x2d, pos, b, s, an, w_in, cqn, w_uq, ckvn, w_ukv, qn, kn, sbn, mlan, w_o, fn,
           w_pq, sub_keys, pu, pv):
    bf = jnp.bfloat16
    row = lambda a: a.reshape(1, -1)
    kpe_cols = jnp.pad(w_in[:, _C_KPE:_C_KPE + MLA_ROPE_DIM],
                       ((0, 0), (MLA_NOPE_DIM, HEAD_PAD - MLA_QK_DIM)))
    win = jnp.concatenate([w_in[:, :_C_KPE], kpe_cols], axis=1).astype(bf)
    wuq = _pad_heads(w_uq, MLA_QK_DIM).astype(bf)
    w_ukv3 = w_ukv.reshape(MLA_KV_RANK, MLA_HEADS, MLA_NOPE_DIM + MLA_V_DIM)
    wuk = _pad_heads(w_ukv3[:, :, :MLA_NOPE_DIM].reshape(MLA_KV_RANK, -1), MLA_NOPE_DIM).astype(bf)
    wuv = w_ukv3[:, :, MLA_NOPE_DIM:].reshape(MLA_KV_RANK, -1).astype(bf)
    pad_gain = lambda g: jnp.pad(g, (0, HEAD_PAD - MLA_QK_DIM)).reshape(1, HEAD_PAD)
    half = MLA_ROPE_DIM // 2
    inv_freq = 1.0 / (ROPE_THETA ** (jnp.arange(half, dtype=jnp.float32) * (2.0 / MLA_ROPE_DIM)))
    freq = jnp.concatenate([jnp.zeros((MLA_NOPE_DIM,), jnp.float32), inv_freq, inv_freq,
                            jnp.zeros((HEAD_PAD - MLA_QK_DIM,), jnp.float32)]).reshape(1, HEAD_PAD)

    sbq, sbk, sbv, q, k, v = _prep_call(
        x2d, pos, row(an), win, row(cqn), wuq, row(ckvn), wuk, wuv,
        pad_gain(qn), pad_gain(kn), freq)

    r3 = lambda a: a.reshape(b, s, a.shape[-1])
    sb = _sb_call(r3(sbq), r3(sbk), r3(sbv)).reshape(b * s, SB_WIDTH)
    vt = v.reshape(b, s // ATT_BLOCK, MLA_WIDTH, ATT_BLOCK)
    mla = _mla_call(r3(q), r3(k), vt).reshape(b * s, MLA_WIDTH)

    keys = sub_keys.transpose(1, 0, 2, 3).reshape(2 * PEER_HEADS, PEER_N_KEYS, PEER_HALF_DIM).astype(bf)
    x2, hn, idx_t, gate_t = _route_call(
        x2d, sb, mla, row(sbn), row(mlan), w_o.astype(bf), row(fn), w_pq.astype(bf), keys)

    slab = lambda w: w.reshape(PEER_N_EXPERTS, D_MODEL // 128, 128)
    uv = jnp.concatenate([slab(pu), slab(pv)], axis=1).reshape(PEER_N_EXPERTS * SLAB_ROWS, 128)
    return _expert_call(idx_t.T, uv, hn, x2, gate_t)


def kernel(x, positions, attn_norm, w_in, cq_norm, w_uq, ckv_norm, w_ukv, q_norm, k_norm,
           sb_out_norm, mla_out_norm, w_o, ffn_norm, peer_w_q, peer_sub_keys, peer_u, peer_v):
    b, s, d = x.shape
    x2d = x.reshape(b * s, d)
    pos = positions.reshape(b * s, 1)
    for l in range(attn_norm.shape[0]):
        x2d = _layer(x2d, pos, b, s, attn_norm[l], w_in[l], cq_norm[l], w_uq[l], ckv_norm[l],
                     w_ukv[l], q_norm[l], k_norm[l], sb_out_norm[l], mla_out_norm[l], w_o[l],
                     ffn_norm[l], peer_w_q[l], peer_sub_keys[l], peer_u[l], peer_v[l])
    return x2d.reshape(b, s, d)
```

```python
import functools
import math

import jax
import jax.numpy as jnp
from jax import lax
from jax.experimental import pallas as pl
from jax.experimental.pallas import tpu as pltpu

D_MODEL = 1024
CHUNK = 64
EPS = 1e-6

SB_HEADS = 8
SB_HEAD_DIM = 64
SB_WIDTH = SB_HEADS * SB_HEAD_DIM

MLA_HEADS = 8
MLA_NOPE_DIM = 64
MLA_ROPE_DIM = 32
MLA_QK_DIM = MLA_NOPE_DIM + MLA_ROPE_DIM
MLA_V_DIM = 64
MLA_Q_RANK = 384
MLA_KV_RANK = 256
MLA_WIDTH = MLA_HEADS * MLA_V_DIM
ROPE_THETA = 10000.0
HEAD_PAD = 128
MLA_PAD_WIDTH = MLA_HEADS * HEAD_PAD

PEER_HEADS = 8
PEER_N_KEYS = 128
PEER_N_EXPERTS = PEER_N_KEYS * PEER_N_KEYS
PEER_KEY_DIM = 256
PEER_HALF_DIM = PEER_KEY_DIM // 2
PEER_TOPK = 16
PEER_PICKS = PEER_HEADS * PEER_TOPK

_C_SBQ = 0
_C_SBK = SB_WIDTH
_C_SBV = 2 * SB_WIDTH
_C_CQ = 3 * SB_WIDTH
_C_CKV = _C_CQ + MLA_Q_RANK
_C_KPE = _C_CKV + MLA_KV_RANK
IN_PAD_WIDTH = _C_KPE + HEAD_PAD

VMEM_LIMIT = 56 * 1024 * 1024

PREP_TOKENS = 512
ATT_BLOCK = 256
ATT_HEAD_GROUP = 8
MLA_HEAD_GROUP = 8
EXPERT_TOKENS = 256
GATHER_UNROLL = 4
GATHER_AHEAD = 3
GATHER_SLOTS = (GATHER_AHEAD + 1) * GATHER_UNROLL
SLAB_ROWS = 2 * D_MODEL // 128
SLAB_PITCH = SLAB_ROWS + 1
SLOT_ROWS = PEER_PICKS * SLAB_PITCH

SB_DEAD_LOG = -105.0


def _rms(x, g):
    ms = jnp.mean(x * x, axis=-1, keepdims=True)
    return x * lax.rsqrt(ms + EPS) * g


def _prep_kernel(x_ref, pos_ref, an_ref, win_ref, cqn_ref, wuq_ref, ckvn_ref,
                 wuk_ref, wuv_ref, qn_ref, kn_ref, freq_ref,
                 sbq_ref, sbk_ref, sbv_ref, q_ref, k_ref, vt_ref):
    h = _rms(x_ref[...], an_ref[...]).astype(jnp.bfloat16)
    proj = jnp.dot(h, win_ref[...], preferred_element_type=jnp.float32)
    sbq_ref[...] = (proj[:, _C_SBQ:_C_SBQ + SB_WIDTH] * (SB_HEAD_DIM ** -0.5)).astype(jnp.bfloat16)
    sbk_ref[...] = proj[:, _C_SBK:_C_SBK + SB_WIDTH].astype(jnp.bfloat16)
    sbv_ref[...] = proj[:, _C_SBV:_C_SBV + SB_WIDTH].astype(jnp.bfloat16)

    cq = _rms(proj[:, _C_CQ:_C_CQ + MLA_Q_RANK], cqn_ref[...]).astype(jnp.bfloat16)
    ckv = _rms(proj[:, _C_CKV:_C_CKV + MLA_KV_RANK], ckvn_ref[...]).astype(jnp.bfloat16)
    kpe = proj[:, _C_KPE:_C_KPE + HEAD_PAD]
    qf = jnp.dot(cq, wuq_ref[...], preferred_element_type=jnp.float32)
    kf = jnp.dot(ckv, wuk_ref[...], preferred_element_type=jnp.float32)
    vf = jnp.dot(ckv, wuv_ref[...], preferred_element_type=jnp.float32)
    for blk in range(vt_ref.shape[0]):
        vt_ref[blk] = vf[blk * ATT_BLOCK:(blk + 1) * ATT_BLOCK, :].T.astype(jnp.bfloat16)

    ang = pos_ref[...].astype(jnp.float32) * freq_ref[...]
    cos = jnp.cos(ang)
    sin = jnp.sin(ang)
    lane = lax.broadcasted_iota(jnp.int32, ang.shape, 1)
    half = MLA_ROPE_DIM // 2
    lo = (lane >= MLA_NOPE_DIM) & (lane < MLA_NOPE_DIM + half)
    hi = (lane >= MLA_NOPE_DIM + half) & (lane < MLA_QK_DIM)
    sin_lo = jnp.where(lo, -sin, 0.0)
    sin_hi = jnp.where(hi, sin, 0.0)

    def norm_rope(xh, g):
        ms = jnp.sum(xh * xh, axis=-1, keepdims=True) * (1.0 / MLA_QK_DIM)
        xn = xh * lax.rsqrt(ms + EPS) * g
        return (xn * cos + pltpu.roll(xn, HEAD_PAD - half, 1) * sin_lo
                + pltpu.roll(xn, half, 1) * sin_hi)

    for hd in range(MLA_HEADS):
        sl = slice(hd * HEAD_PAD, (hd + 1) * HEAD_PAD)
        q_ref[:, sl] = norm_rope(qf[:, sl], qn_ref[...]).astype(jnp.bfloat16)
        k_ref[:, sl] = norm_rope(kf[:, sl] + kpe, kn_ref[...]).astype(jnp.bfloat16)


def _prep_call(x2d, pos, an, win, cqn, wuq, ckvn, wuk, wuv, qn, kn, freq):
    n = x2d.shape[0]
    t = PREP_TOKENS
    full = lambda a: pl.BlockSpec(a.shape, lambda i: (0,) * a.ndim)
    tok = lambda w: pl.BlockSpec((t, w), lambda i: (i, 0))
    bf = jnp.bfloat16
    return pl.pallas_call(
        _prep_kernel,
        grid=(n // t,),
        in_specs=[tok(D_MODEL), tok(1), full(an), full(win), full(cqn), full(wuq),
                  full(ckvn), full(wuk), full(wuv), full(qn), full(kn), full(freq)],
        out_specs=[tok(SB_WIDTH), tok(SB_WIDTH), tok(SB_WIDTH),
                   tok(MLA_PAD_WIDTH), tok(MLA_PAD_WIDTH),
                   pl.BlockSpec((t // ATT_BLOCK, MLA_WIDTH, ATT_BLOCK), lambda i: (i, 0, 0))],
        out_shape=[jax.ShapeDtypeStruct((n, SB_WIDTH), bf)] * 3
        + [jax.ShapeDtypeStruct((n, MLA_PAD_WIDTH), bf)] * 2
        + [jax.ShapeDtypeStruct((n // ATT_BLOCK, MLA_WIDTH, ATT_BLOCK), bf)],
        compiler_params=pltpu.CompilerParams(
            dimension_semantics=("parallel",), vmem_limit_bytes=VMEM_LIMIT),
        name="prep",
    )(x2d, pos, an, win, cqn, wuq, ckvn, wuk, wuv, qn, kn, freq)


def _sb_kernel(q_ref, k_ref, v_ref, o_ref):
    i = pl.program_id(1)
    tb = ATT_BLOCK
    row = lax.broadcasted_iota(jnp.int32, (tb, tb), 0)
    col = lax.broadcasted_iota(jnp.int32, (tb, tb), 1)
    strict = col < row
    upper = jnp.where(row > col, 1.0, 0.0).astype(jnp.bfloat16)

    for grp in range(SB_HEADS // ATT_HEAD_GROUP):
        heads = range(grp * ATT_HEAD_GROUP, (grp + 1) * ATT_HEAD_GROUP)

        def walk(kb, st, masked):
            start = pl.multiple_of(kb * tb, tb)
            heads_sl = [slice(hd * SB_HEAD_DIM, (hd + 1) * SB_HEAD_DIM) for hd in heads]
            zs = [lax.dot_general(q_ref[0, :, sl], k_ref[0, pl.ds(start, tb), sl],
                                  (((1,), (1,)), ((), ())), preferred_element_type=jnp.float32)
                  for sl in heads_sl]
            mids = []
            for z in zs:
                lf_all = -(jnp.maximum(z, 0.0) + jnp.log(1.0 + jnp.exp(-jnp.abs(z))))
                lf = jnp.where(strict, lf_all, 0.0) if masked else lf_all
                lf_hi = lf.astype(jnp.bfloat16)
                lf_lo = (lf - lf_hi.astype(jnp.float32)).astype(jnp.bfloat16)
                mids.append((z + lf_all, lf_hi, lf_lo, jnp.sum(lf, axis=-1, keepdims=True)))
            laters = [jnp.dot(hi, upper, preferred_element_type=jnp.float32)
                      + jnp.dot(lo, upper, preferred_element_type=jnp.float32)
                      for _, hi, lo, _ in mids]
            weights = []
            for (ls, _, _, _), later, (carry, _) in zip(mids, laters, st):
                a = jnp.exp(ls + later + carry)
                if masked:
                    a = jnp.where(strict, a, 0.0)
                weights.append(a.astype(jnp.bfloat16))
            return [(carry + rs,
                     acc + jnp.dot(a, v_ref[0, pl.ds(start, tb), sl],
                                   preferred_element_type=jnp.float32))
                    for a, sl, (_, _, _, rs), (carry, acc) in zip(weights, heads_sl, mids, st)]

        carry0 = jnp.zeros((tb, 1), jnp.float32)
        acc0 = jnp.zeros((tb, SB_HEAD_DIM), jnp.float32)
        state = walk(i, [(carry0, acc0)] * len(heads), True)

        def alive(st):
            top = st[0][0]
            for c, _ in st[1:]:
                top = jnp.maximum(top, c)
            return jnp.max(top) > SB_DEAD_LOG

        def cond(s):
            return (s[0] < i) & s[1]

        def body(s):
            j, _, st = s
            st = walk(i - 1 - j, st, False)
            return j + 1, alive(st), st

        _, _, state = lax.while_loop(cond, body, (0, alive(state), state))
        for hd, (_, acc) in zip(heads, state):
            o_ref[0, :, hd * SB_HEAD_DIM:(hd + 1) * SB_HEAD_DIM] = acc


def _sb_call(q, k, v):
    b, s, w = q.shape
    tb = ATT_BLOCK
    return pl.pallas_call(
        _sb_kernel,
        grid=(b, s // tb),
        in_specs=[pl.BlockSpec((1, tb, w), lambda bi, i: (bi, i, 0)),
                  pl.BlockSpec((1, s, w), lambda bi, i: (bi, 0, 0)),
                  pl.BlockSpec((1, s, w), lambda bi, i: (bi, 0, 0))],
        out_specs=pl.BlockSpec((1, tb, w), lambda bi, i: (bi, i, 0)),
        out_shape=jax.ShapeDtypeStruct((b, s, w), jnp.float32),
        compiler_params=pltpu.CompilerParams(
            dimension_semantics=("parallel", "arbitrary"), vmem_limit_bytes=VMEM_LIMIT),
        name="sb_attn",
    )(q, k, v)


def _mla_kernel(q_ref, k_ref, vt_ref, o_ref):
    i = pl.program_id(1)
    tb = ATT_BLOCK
    key = lax.broadcasted_iota(jnp.int32, (tb, tb), 0)
    qry = lax.broadcasted_iota(jnp.int32, (tb, tb), 1)
    allowed = (key // CHUNK) <= (qry // CHUNK)
    scale = MLA_QK_DIM ** -0.5

    def scores(hd, kb):
        qh = q_ref[0, :, hd * HEAD_PAD:(hd + 1) * HEAD_PAD]
        start = pl.multiple_of(kb * tb, tb)
        ks = k_ref[0, pl.ds(start, tb), hd * HEAD_PAD:(hd + 1) * HEAD_PAD]
        vt = vt_ref[0, kb, hd * MLA_V_DIM:(hd + 1) * MLA_V_DIM, :]
        st = lax.dot_general(ks, qh, (((1,), (1,)), ((), ())),
                             preferred_element_type=jnp.float32) * scale
        return st, vt

    outs = []
    for grp in range(MLA_HEADS // MLA_HEAD_GROUP):
        heads = range(grp * MLA_HEAD_GROUP, (grp + 1) * MLA_HEAD_GROUP)

        def walk(kb, st, masked):
            tiles = [scores(hd, kb) for hd in heads]
            mids = []
            for (s_t, vt), (m, l, acc) in zip(tiles, st):
                if masked:
                    s_t = jnp.where(allowed, s_t, -jnp.inf)
                m_new = jnp.maximum(m, jnp.max(s_t, axis=0, keepdims=True))
                alpha = jnp.exp(m - m_new)
                p = jnp.exp(s_t - m_new)
                l = alpha * l + jnp.sum(p, axis=0, keepdims=True)
                mids.append((m_new, l, alpha, p.astype(jnp.bfloat16), vt, acc))
            return [(m, l, alpha * acc + jnp.dot(vt, p, preferred_element_type=jnp.float32))
                    for m, l, alpha, p, vt, acc in mids]

        empty = (jnp.full((1, tb), -jnp.inf, jnp.float32), jnp.zeros((1, tb), jnp.float32),
                 jnp.zeros((MLA_V_DIM, tb), jnp.float32))
        state = walk(i, [empty] * len(heads), True)
        state = lax.fori_loop(0, i, lambda kb, st: walk(kb, st, False), state)
        outs += [acc / l for _, l, acc in state]
    o_ref[0] = jnp.concatenate(outs, axis=0).T


def _mla_call(q, k, vt):
    b, s, wq = q.shape
    _, nblk, wv, tb = vt.shape
    return pl.pallas_call(
        _mla_kernel,
        grid=(b, s // tb),
        in_specs=[pl.BlockSpec((1, tb, wq), lambda bi, i: (bi, i, 0)),
                  pl.BlockSpec((1, s, wq), lambda bi, i: (bi, 0, 0)),
                  pl.BlockSpec((1, nblk, wv, tb), lambda bi, i: (bi, 0, 0, 0))],
        out_specs=pl.BlockSpec((1, tb, wv), lambda bi, i: (bi, i, 0)),
        out_shape=jax.ShapeDtypeStruct((b, s, wv), jnp.float32),
        compiler_params=pltpu.CompilerParams(
            dimension_semantics=("parallel", "arbitrary"), vmem_limit_bytes=VMEM_LIMIT),
        name="mla_attn",
    )(q, k, vt)


def _topk_rows(a, k, ids=None):
    r, t = a.shape
    rows = lax.broadcasted_iota(jnp.int32, (r, t), 0) if ids is None else ids
    krow = lax.broadcasted_iota(jnp.int32, (k, t), 0)
    big = jnp.iinfo(jnp.int32).max

    def body(it, c):
        a, vals, idxs = c
        m = jnp.max(a, axis=0, keepdims=True)
        am = jnp.min(jnp.where(a == m, rows, big), axis=0, keepdims=True)
        a = jnp.where(rows == am, -jnp.inf, a)
        vals = jnp.where(krow == it, m, vals)
        idxs = jnp.where(krow == it, am, idxs)
        return a, vals, idxs

    _, vals, idxs = lax.fori_loop(
        0, k, body, (a, jnp.zeros((k, t), jnp.float32), jnp.zeros((k, t), jnp.int32)))
    return vals, idxs


def _select_rows(sel, table):
    out = jnp.zeros(sel.shape, table.dtype)
    for r in range(table.shape[0]):
        out = jnp.where(sel == r, table[r:r + 1, :], out)
    return out


def _candidate_tiles(k):
    tiles = [("row", 0, s) for s in range(0, k, 8)] + [("col", 0, s) for s in range(0, k, 8)]
    f = 1
    while (f + 1) * (f + 1) <= k:
        tiles += [("row", f, 0), ("col", f, 0)]
        f += 1
    seen, used, ids = set(), [], []
    for kind, fixed, start in tiles:
        tile_ids = []
        for r in range(8):
            pair = (fixed, start + r) if kind == "row" else (start + r, fixed)
            ok = (pair[0] + 1) * (pair[1] + 1) <= k and pair not in seen
            seen.add(pair)
            tile_ids.append(pair[0] * k + pair[1] if ok else -1)
        if max(tile_ids) >= 0:
            used.append((kind, fixed, start))
            ids += tile_ids
    want = sum(1 for a in range(k) for b in range(k) if (a + 1) * (b + 1) <= k)
    assert sum(i >= 0 for i in ids) == want, "candidate tiles must cover the whole staircase"
    return used, ids


_CAND_TILES, _CAND_IDS = _candidate_tiles(PEER_TOPK)


def _route_kernel(x_ref, sb_ref, mla_ref, sbn_ref, mlan_ref, wo_ref, fn_ref, wq_ref,
                  keys_ref, x2_ref, hn_ref, score_ref):
    bf = jnp.bfloat16
    sbn = _rms(sb_ref[...], sbn_ref[...]).astype(bf)
    mlan = _rms(mla_ref[...], mlan_ref[...]).astype(bf)
    attn = (jnp.dot(sbn, wo_ref[0:SB_WIDTH, :], preferred_element_type=jnp.float32)
            + jnp.dot(mlan, wo_ref[SB_WIDTH:SB_WIDTH + MLA_WIDTH, :],
                      preferred_element_type=jnp.float32))
    x2 = x_ref[...] + attn
    x2_ref[...] = x2
    hn = _rms(x2, fn_ref[...])
    hn_ref[...] = hn
    qp = jnp.dot(hn.astype(bf), wq_ref[...], preferred_element_type=jnp.float32)
    for hc in range(2 * PEER_HEADS):
        qhc = qp[:, hc * PEER_HALF_DIM:(hc + 1) * PEER_HALF_DIM].astype(bf)
        score_ref[0, hc] = lax.dot_general(keys_ref[hc], qhc, (((1,), (1,)), ((), ())),
                                           preferred_element_type=jnp.float32)


def _route_call(x2d, sb, mla, sbn, mlan, wo, fn, wq, keys):
    n = x2d.shape[0]
    t = EXPERT_TOKENS
    full = lambda a: pl.BlockSpec(a.shape, lambda i: (0,) * a.ndim)
    tok = lambda w: pl.BlockSpec((t, w), lambda i: (i, 0))
    return pl.pallas_call(
        _route_kernel,
        grid=(n // t,),
        in_specs=[tok(D_MODEL), tok(SB_WIDTH), tok(MLA_WIDTH), full(sbn), full(mlan),
                  full(wo), full(fn), full(wq), full(keys)],
        out_specs=[tok(D_MODEL), tok(D_MODEL),
                   pl.BlockSpec((1, 2 * PEER_HEADS, PEER_N_KEYS, t), lambda i: (i, 0, 0, 0))],
        out_shape=[jax.ShapeDtypeStruct((n, D_MODEL), jnp.float32),
                   jax.ShapeDtypeStruct((n, D_MODEL), jnp.float32),
                   jax.ShapeDtypeStruct((n // t, 2 * PEER_HEADS, PEER_N_KEYS, t), jnp.float32)],
        compiler_params=pltpu.CompilerParams(
            dimension_semantics=("parallel",), vmem_limit_bytes=VMEM_LIMIT),
        name="route",
    )(x2d, sb, mla, sbn, mlan, wo, fn, wq, keys)


def _expert_kernel(uv_hbm, hn_ref, x2_ref, sc0_ref, scn_ref, ids_ref, o_ref,
                   idx_vm, idx_smem, gate_vm, buf_even, buf_odd, idx_sem, row_sem):
    step = pl.program_id(0)
    nsteps = pl.num_programs(0)
    tb = EXPERT_TOKENS
    nu = GATHER_UNROLL
    ahead_b = GATHER_AHEAD
    per_buf = GATHER_SLOTS // 2
    bufs = (buf_even, buf_odd)
    head_bunches = tb // nu // PEER_HEADS
    assert ahead_b % 2 == 1 and GATHER_SLOTS == (ahead_b + 1) * nu
    assert (tb // nu) % (2 * (per_buf // nu)) == 0
    assert head_bunches % 2 == 0 and head_bunches >= 7
    cur = step % 2
    has_next = step + 1 < nsteps
    k = PEER_TOPK

    def candidates(s0, s1):
        pieces = []
        for kind, fixed, start in _CAND_TILES:
            if kind == "row":
                pieces.append(s0[fixed:fixed + 1, :] + s1[start:start + 8, :])
            else:
                pieces.append(s0[start:start + 8, :] + s1[fixed:fixed + 1, :])
        return jnp.where(ids_ref[...] >= 0, jnp.concatenate(pieces, axis=0), -jnp.inf)

    def finish(hd, slot, best, pos, i0, i1):
        e0 = _select_rows(pos // k, i0)
        e1 = _select_rows(pos % k, i1)
        p = jnp.exp(best - best[0:1, :])
        gate = p / jnp.sum(p, axis=0, keepdims=True)
        rows = pl.ds(hd * k if isinstance(hd, int) else pl.multiple_of(hd * k, k), k)
        idx_vm[rows, :] = e0 * PEER_N_KEYS + e1
        for tile in range(tb // 128):
            gate_vm[slot, tile, rows, :] = gate[:, tile * 128:(tile + 1) * 128]

    def topk_steps(state, its, ids=None):
        a, vals, idxs = state
        rows = lax.broadcasted_iota(jnp.int32, a.shape, 0) if ids is None else ids
        krow = lax.broadcasted_iota(jnp.int32, vals.shape, 0)
        big = jnp.iinfo(jnp.int32).max
        for it in its:
            m = jnp.max(a, axis=0, keepdims=True)
            am = jnp.min(jnp.where(a == m, rows, big), axis=0, keepdims=True)
            a = jnp.where(rows == am, -jnp.inf, a)
            vals = jnp.where(krow == it, m, vals)
            idxs = jnp.where(krow == it, am, idxs)
        return a, vals, idxs

    def fresh(a):
        return a, jnp.zeros((k, tb), jnp.float32), jnp.zeros((k, tb), jnp.int32)

    def idx_copy():
        return pltpu.make_async_copy(idx_vm, idx_smem, idx_sem)

    def place(b, j):
        par, half = b
        return par, (half % (per_buf // nu)) * nu + j

    def start_token(t, par, slot):
        base = slot * SLOT_ROWS
        for p in range(PEER_PICKS):
            src = pl.multiple_of(idx_smem[p, t] * SLAB_ROWS, SLAB_ROWS)
            pltpu.make_async_copy(
                uv_hbm.at[pl.ds(src, SLAB_ROWS), :],
                bufs[par].at[pl.ds(base + p * SLAB_PITCH, SLAB_ROWS), :],
                row_sem.at[par * per_buf + slot]).start(priority=p % 2)

    def wait_token(par, slot):
        rows = PEER_PICKS * SLAB_ROWS
        pltpu.make_async_copy(uv_hbm.at[pl.ds(0, rows), :],
                              bufs[par].at[pl.ds(slot * SLOT_ROWS, rows), :],
                              row_sem.at[par * per_buf + slot]).wait()

    @pl.when(step == 0)
    def _():
        def head(hd, _):
            s0, i0 = _topk_rows(sc0_ref[0, 2 * hd], k)
            s1, i1 = _topk_rows(sc0_ref[0, 2 * hd + 1], k)
            best, pos = _topk_rows(candidates(s0, s1), k, ids_ref[...])
            finish(hd, 0, best, pos, i0, i1)
            return 0

        lax.fori_loop(0, PEER_HEADS, head, 0)
        idx_copy().start()

    idx_copy().wait()
    for b in range(ahead_b):
        for j in range(nu):
            start_token(b * nu + j, *place((b % 2, b // 2), j))

    lane = lax.broadcasted_iota(jnp.int32, (PEER_PICKS, 128), 1)
    groups = PEER_PICKS // 8
    tiles = D_MODEL // 128

    def compute_token(t, par, slot):
        buf = bufs[par]
        base = slot * SLOT_ROWS
        xrow = hn_ref[pl.ds(t, 1), :]
        pres = []
        for g in range(groups):
            acc = None
            for c in range(tiles):
                u = buf[pl.ds(base + g * 8 * SLAB_PITCH + c, 8, stride=SLAB_PITCH), :]
                term = u * xrow[:, c * 128:(c + 1) * 128]
                acc = term if acc is None else acc + term
            pres.append(jnp.sum(acc, axis=1, keepdims=True))
        pre = jnp.concatenate(pres, axis=0)
        gate = jnp.sum(jnp.where(lane == t % 128, gate_vm[cur, t // 128], 0.0),
                       axis=1, keepdims=True)
        coef = gate * (0.5 * pre * (1.0 + lax.erf(pre * (2.0 ** -0.5))))
        outs = []
        for c in range(tiles):
            acc = None
            for g in range(groups):
                v = buf[pl.ds(base + g * 8 * SLAB_PITCH + tiles + c, 8, stride=SLAB_PITCH), :]
                term = coef[g * 8:(g + 1) * 8, :] * v
                acc = term if acc is None else acc + term
            outs.append(jnp.sum(acc, axis=0, keepdims=True))
        out = jnp.concatenate(outs, axis=1)
        o_ref[pl.ds(t, 1), :] = x2_ref[pl.ds(t, 1), :] + out

    def bunch(par, half, refill, search=None):
        first = (2 * half + par) * nu
        for j in range(nu):
            wait_token(*place((par, half), j))
        found = search() if search is not None else None
        for j in range(nu):
            compute_token(first + j, *place((par, half), j))

        if refill:
            later = (par + ahead_b) // 2 + half
            for j in range(nu):
                start_token(first + ahead_b * nu + j, *place((1 - par, later), j))
        return found

    half_its = (range(0, k // 2), range(k // 2, k))

    def head_body(hd, last):
        def run(j, search=None):
            return bunch(j % 2, hd * (head_bunches // 2) + j // 2,
                         not last or j < head_bunches - ahead_b, search)

        st0 = fresh(scn_ref[0, 2 * hd])
        st1 = fresh(scn_ref[0, 2 * hd + 1])
        st0 = run(0, lambda: topk_steps(st0, half_its[0]))
        st0 = run(1, lambda: topk_steps(st0, half_its[1]))
        st1 = run(2, lambda: topk_steps(st1, half_its[0]))
        st1 = run(3, lambda: topk_steps(st1, half_its[1]))
        ids = ids_ref[...]
        st2 = fresh(candidates(st0[1], st1[1]))
        st2 = run(4, lambda: topk_steps(st2, half_its[0], ids))
        st2 = run(5, lambda: topk_steps(st2, half_its[1], ids))
        run(6, lambda: finish(hd, 1 - cur, st2[1], st2[2], st0[2], st1[2]))
        for j in range(7, head_bunches):
            run(j)
        return 0

    lax.fori_loop(0, PEER_HEADS - 1, lambda hd, c: head_body(hd, False), 0)
    head_body(PEER_HEADS - 1, True)

    @pl.when(has_next)
    def _():
        idx_copy().start()


def _expert_call(uv, hn, x2, scores):
    n = hn.shape[0]
    tb = EXPERT_TOKENS
    nblk = n // tb
    tok = pl.BlockSpec((tb, D_MODEL), lambda i: (i, 0))
    sc_shape = (1,) + scores.shape[1:]
    ids = jnp.broadcast_to(jnp.asarray(_CAND_IDS, jnp.int32)[:, None], (len(_CAND_IDS), tb))
    return pl.pallas_call(
        _expert_kernel,
        grid=(nblk,),
        in_specs=[pl.BlockSpec(memory_space=pl.ANY),
                  tok, tok,
                  pl.BlockSpec(sc_shape, lambda i: (0, 0, 0, 0)),
                  pl.BlockSpec(sc_shape, lambda i: (jnp.minimum(i + 1, nblk - 1), 0, 0, 0)),
                  pl.BlockSpec(ids.shape, lambda i: (0, 0))],
        out_specs=tok,
        out_shape=jax.ShapeDtypeStruct((n, D_MODEL), jnp.float32),
        scratch_shapes=[pltpu.VMEM((PEER_PICKS, tb), jnp.int32),
                        pltpu.SMEM((PEER_PICKS, tb), jnp.int32),
                        pltpu.VMEM((2, tb // 128, PEER_PICKS, 128), jnp.float32),
                        pltpu.VMEM((GATHER_SLOTS // 2 * SLOT_ROWS, 128), jnp.float32),
                        pltpu.VMEM((GATHER_SLOTS // 2 * SLOT_ROWS, 128), jnp.float32),
                        pltpu.SemaphoreType.DMA,
                        pltpu.SemaphoreType.DMA((GATHER_SLOTS,))],
        compiler_params=pltpu.CompilerParams(
            dimension_semantics=("arbitrary",), vmem_limit_bytes=VMEM_LIMIT),
        name="experts",
    )(uv, hn, x2, scores, scores, ids)


def _pad_heads(w, head_dim):
    k = w.shape[0]
    w = w.reshape(k, -1, head_dim)
    w = jnp.pad(w, ((0, 0), (0, 0), (0, HEAD_PAD - head_dim)))
    return w.reshape(k, -1)


def _layer(x2d, pos, b, s, an, w_in, cqn, w_uq, ckvn, w_ukv, qn, kn, sbn, mlan, w_o, fn,
           w_pq, sub_keys, pu, pv):
    bf = jnp.bfloat16
    row = lambda a: a.reshape(1, -1)
    kpe_cols = jnp.pad(w_in[:, _C_KPE:_C_KPE + MLA_ROPE_DIM],
                       ((0, 0), (MLA_NOPE_DIM, HEAD_PAD - MLA_QK_DIM)))
    win = jnp.concatenate([w_in[:, :_C_KPE], kpe_cols], axis=1).astype(bf)
    wuq = _pad_heads(w_uq, MLA_QK_DIM).astype(bf)
    w_ukv3 = w_ukv.reshape(MLA_KV_RANK, MLA_HEADS, MLA_NOPE_DIM + MLA_V_DIM)
    wuk = _pad_heads(w_ukv3[:, :, :MLA_NOPE_DIM].reshape(MLA_KV_RANK, -1), MLA_NOPE_DIM).astype(bf)
    wuv = w_ukv3[:, :, MLA_NOPE_DIM:].reshape(MLA_KV_RANK, -1).astype(bf)
    pad_gain = lambda g: jnp.pad(g, (0, HEAD_PAD - MLA_QK_DIM)).reshape(1, HEAD_PAD)
    half = MLA_ROPE_DIM // 2
    inv_freq = 1.0 / (ROPE_THETA ** (jnp.arange(half, dtype=jnp.float32) * (2.0 / MLA_ROPE_DIM)))
    freq = jnp.concatenate([jnp.zeros((MLA_NOPE_DIM,), jnp.float32), inv_freq, inv_freq,
                            jnp.zeros((HEAD_PAD - MLA_QK_DIM,), jnp.float32)]).reshape(1, HEAD_PAD)

    sbq, sbk, sbv, q, k, v = _prep_call(
        x2d, pos, row(an), win, row(cqn), wuq, row(ckvn), wuk, wuv,
        pad_gain(qn), pad_gain(kn), freq)

    r3 = lambda a: a.reshape(b, s, a.shape[-1])
    sb = _sb_call(r3(sbq), r3(sbk), r3(sbv)).reshape(b * s, SB_WIDTH)
    vt = v.reshape(b, s // ATT_BLOCK, MLA_WIDTH, ATT_BLOCK)
    mla = _mla_call(r3(q), r3(k), vt).reshape(b * s, MLA_WIDTH)

    keys = sub_keys.transpose(1, 0, 2, 3).reshape(2 * PEER_HEADS, PEER_N_KEYS, PEER_HALF_DIM).astype(bf)
    x2, hn, scores = _route_call(
        x2d, sb, mla, row(sbn), row(mlan), w_o.astype(bf), row(fn), w_pq.astype(bf), keys)

    slab = lambda w: w.reshape(PEER_N_EXPERTS, D_MODEL // 128, 128)
    uv = jnp.concatenate([slab(pu), slab(pv)], axis=1).reshape(PEER_N_EXPERTS * SLAB_ROWS, 128)
    return _expert_call(uv, hn, x2, scores)


def kernel(x, positions, attn_norm, w_in, cq_norm, w_uq, ckv_norm, w_ukv, q_norm, k_norm,
           sb_out_norm, mla_out_norm, w_o, ffn_norm, peer_w_q, peer_sub_keys, peer_u, peer_v):
    b, s, d = x.shape
    x2d = x.reshape(b * s, d)
    pos = positions.reshape(b * s, 1)
    for l in range(attn_norm.shape[0]):
        x2d = _layer(x2d, pos, b, s, attn_norm[l], w_in[l], cq_norm[l], w_uq[l], ckv_norm[l],
                     w_ukv[l], q_norm[l], k_norm[l], sb_out_norm[l], mla_out_norm[l], w_o[l],
                     ffn_norm[l], peer_w_q[l], peer_sub_keys[l], peer_u[l], peer_v[l])
    return x2d.reshape(b, s, d)
```

```python
import functools
import math

import jax
import jax.numpy as jnp
from jax import lax
from jax.experimental import pallas as pl
from jax.experimental.pallas import tpu as pltpu

D_MODEL = 1024
CHUNK = 64
EPS = 1e-6

SB_HEADS = 8
SB_HEAD_DIM = 64
SB_WIDTH = SB_HEADS * SB_HEAD_DIM

MLA_HEADS = 8
MLA_NOPE_DIM = 64
MLA_ROPE_DIM = 32
MLA_QK_DIM = MLA_NOPE_DIM + MLA_ROPE_DIM
MLA_V_DIM = 64
MLA_Q_RANK = 384
MLA_KV_RANK = 256
MLA_WIDTH = MLA_HEADS * MLA_V_DIM
ROPE_THETA = 10000.0
HEAD_PAD = 128
MLA_PAD_WIDTH = MLA_HEADS * HEAD_PAD

PEER_HEADS = 8
PEER_N_KEYS = 128
PEER_N_EXPERTS = PEER_N_KEYS * PEER_N_KEYS
PEER_KEY_DIM = 256
PEER_HALF_DIM = PEER_KEY_DIM // 2
PEER_TOPK = 16
PEER_PICKS = PEER_HEADS * PEER_TOPK

_C_SBQ = 0
_C_SBK = SB_WIDTH
_C_SBV = 2 * SB_WIDTH
_C_CQ = 3 * SB_WIDTH
_C_CKV = _C_CQ + MLA_Q_RANK
_C_KPE = _C_CKV + MLA_KV_RANK
IN_PAD_WIDTH = _C_KPE + HEAD_PAD

VMEM_LIMIT = 56 * 1024 * 1024

PREP_TOKENS = 512
PREP_CHUNK = 128
ATT_BLOCK = 256
ATT_HEAD_GROUP = 8
MLA_HEAD_GROUP = 8
EXPERT_TOKENS = 256
GATHER_UNROLL = 4
GATHER_AHEAD = 3
GATHER_SLOTS = (GATHER_AHEAD + 1) * GATHER_UNROLL
SLAB_ROWS = 2 * D_MODEL // 128
SLAB_PITCH = SLAB_ROWS + 1
SLOT_ROWS = PEER_PICKS * SLAB_PITCH

LOG2E = math.log2(math.e)

SB_DEAD_LOG = -105.0


def _rms(x, g):
    ms = jnp.mean(x * x, axis=-1, keepdims=True)
    return x * lax.rsqrt(ms + EPS) * g


def _prep_kernel(x_ref, pos_ref, an_ref, win_ref, cqn_ref, wuq_ref, ckvn_ref,
                 wuk_ref, wuv_ref, qn_ref, kn_ref, freq_ref,
                 sbq_ref, sbk_ref, sbv_ref, q_ref, k_ref, vt_ref):
    bf = jnp.bfloat16
    half = MLA_ROPE_DIM // 2
    nchunks = PREP_TOKENS // PREP_CHUNK

    def project(c):
        r = pl.ds(c * PREP_CHUNK, PREP_CHUNK)
        proj = jnp.dot(_rms(x_ref[r, :], an_ref[...]).astype(bf), win_ref[...],
                       preferred_element_type=jnp.float32)
        sbq_ref[r, :] = (proj[:, _C_SBQ:_C_SBQ + SB_WIDTH] * (SB_HEAD_DIM ** -0.5)).astype(bf)
        sbk_ref[r, :] = proj[:, _C_SBK:_C_SBK + SB_WIDTH].astype(bf)
        sbv_ref[r, :] = proj[:, _C_SBV:_C_SBV + SB_WIDTH].astype(bf)
        cq = _rms(proj[:, _C_CQ:_C_CQ + MLA_Q_RANK], cqn_ref[...]).astype(bf)
        ckv = _rms(proj[:, _C_CKV:_C_CKV + MLA_KV_RANK], ckvn_ref[...]).astype(bf)
        ups = (jnp.dot(cq, wuq_ref[...], preferred_element_type=jnp.float32),
               jnp.dot(ckv, wuk_ref[...], preferred_element_type=jnp.float32),
               jnp.dot(ckv, wuv_ref[...], preferred_element_type=jnp.float32))
        return r, proj[:, _C_KPE:_C_KPE + HEAD_PAD], ups

    def finish(c, r, kpe, ups):
        qf, kf, vf = ups
        per_blk = ATT_BLOCK // PREP_CHUNK
        vt_ref[c // per_blk, :, (c % per_blk) * PREP_CHUNK:(c % per_blk + 1) * PREP_CHUNK] = (
            vf.T.astype(bf))
        ang = pos_ref[r, :].astype(jnp.float32) * freq_ref[...]
        cos = jnp.cos(ang)
        sin = jnp.sin(ang)
        lane = lax.broadcasted_iota(jnp.int32, ang.shape, 1)
        lo = (lane >= MLA_NOPE_DIM) & (lane < MLA_NOPE_DIM + half)
        hi = (lane >= MLA_NOPE_DIM + half) & (lane < MLA_QK_DIM)
        sin_lo = jnp.where(lo, -sin, 0.0)
        sin_hi = jnp.where(hi, sin, 0.0)

        def norm_rope(xh, g):
            ms = jnp.sum(xh * xh, axis=-1, keepdims=True) * (1.0 / MLA_QK_DIM)
            xn = xh * lax.rsqrt(ms + EPS) * g
            return (xn * cos + pltpu.roll(xn, HEAD_PAD - half, 1) * sin_lo
                    + pltpu.roll(xn, half, 1) * sin_hi)

        for hd in range(MLA_HEADS):
            sl = slice(hd * HEAD_PAD, (hd + 1) * HEAD_PAD)
            q_ref[r, sl] = norm_rope(qf[:, sl], qn_ref[...]).astype(bf)
            k_ref[r, sl] = norm_rope(kf[:, sl] + kpe, kn_ref[...]).astype(bf)

    pending = project(0)
    for c in range(nchunks):
        following = project(c + 1) if c + 1 < nchunks else None
        finish(c, *pending)
        pending = following


def _prep_call(x2d, pos, an, win, cqn, wuq, ckvn, wuk, wuv, qn, kn, freq):
    n = x2d.shape[0]
    t = PREP_TOKENS
    full = lambda a: pl.BlockSpec(a.shape, lambda i: (0,) * a.ndim)
    tok = lambda w: pl.BlockSpec((t, w), lambda i: (i, 0))
    bf = jnp.bfloat16
    return pl.pallas_call(
        _prep_kernel,
        grid=(n // t,),
        in_specs=[tok(D_MODEL), tok(1), full(an), full(win), full(cqn), full(wuq),
                  full(ckvn), full(wuk), full(wuv), full(qn), full(kn), full(freq)],
        out_specs=[tok(SB_WIDTH), tok(SB_WIDTH), tok(SB_WIDTH),
                   tok(MLA_PAD_WIDTH), tok(MLA_PAD_WIDTH),
                   pl.BlockSpec((t // ATT_BLOCK, MLA_WIDTH, ATT_BLOCK), lambda i: (i, 0, 0))],
        out_shape=[jax.ShapeDtypeStruct((n, SB_WIDTH), bf)] * 3
        + [jax.ShapeDtypeStruct((n, MLA_PAD_WIDTH), bf)] * 2
        + [jax.ShapeDtypeStruct((n // ATT_BLOCK, MLA_WIDTH, ATT_BLOCK), bf)],
        compiler_params=pltpu.CompilerParams(
            dimension_semantics=("parallel",), vmem_limit_bytes=VMEM_LIMIT),
        name="prep",
    )(x2d, pos, an, win, cqn, wuq, ckvn, wuk, wuv, qn, kn, freq)


def _sb_kernel(q_ref, k_ref, v_ref, o_ref):
    i = pl.program_id(1)
    tb = ATT_BLOCK
    row = lax.broadcasted_iota(jnp.int32, (tb, tb), 0)
    col = lax.broadcasted_iota(jnp.int32, (tb, tb), 1)
    strict = col < row
    upper = jnp.where(row > col, 1.0, 0.0).astype(jnp.bfloat16)

    for grp in range(SB_HEADS // ATT_HEAD_GROUP):
        heads = range(grp * ATT_HEAD_GROUP, (grp + 1) * ATT_HEAD_GROUP)

        def walk(kb, st, masked):
            start = pl.multiple_of(kb * tb, tb)
            heads_sl = [slice(hd * SB_HEAD_DIM, (hd + 1) * SB_HEAD_DIM) for hd in heads]
            zs = [lax.dot_general(q_ref[0, :, sl], k_ref[0, pl.ds(start, tb), sl],
                                  (((1,), (1,)), ((), ())), preferred_element_type=jnp.float32)
                  for sl in heads_sl]
            mids = []
            for z in zs:
                lf_all = -(jnp.maximum(z, 0.0) + jnp.log(1.0 + jnp.exp(-jnp.abs(z))))
                lf = jnp.where(strict, lf_all, 0.0) if masked else lf_all
                lf_hi = lf.astype(jnp.bfloat16)
                lf_lo = (lf - lf_hi.astype(jnp.float32)).astype(jnp.bfloat16)
                mids.append((z + lf_all, lf_hi, lf_lo, jnp.sum(lf, axis=-1, keepdims=True)))
            laters = [jnp.dot(hi, upper, preferred_element_type=jnp.float32)
                      + jnp.dot(lo, upper, preferred_element_type=jnp.float32)
                      for _, hi, lo, _ in mids]
            weights = []
            for (ls, _, _, _), later, (carry, _) in zip(mids, laters, st):
                a = jnp.exp(ls + later + carry)
                if masked:
                    a = jnp.where(strict, a, 0.0)
                weights.append(a.astype(jnp.bfloat16))
            return [(carry + rs,
                     acc + jnp.dot(a, v_ref[0, pl.ds(start, tb), sl],
                                   preferred_element_type=jnp.float32))
                    for a, sl, (_, _, _, rs), (carry, acc) in zip(weights, heads_sl, mids, st)]

        carry0 = jnp.zeros((tb, 1), jnp.float32)
        acc0 = jnp.zeros((tb, SB_HEAD_DIM), jnp.float32)
        state = walk(i, [(carry0, acc0)] * len(heads), True)

        def alive(st):
            top = st[0][0]
            for c, _ in st[1:]:
                top = jnp.maximum(top, c)
            return jnp.max(top) > SB_DEAD_LOG

        def cond(s):
            return (s[0] < i) & s[1]

        def body(s):
            j, _, st = s
            st = walk(i - 1 - j, st, False)
            return j + 1, alive(st), st

        _, _, state = lax.while_loop(cond, body, (0, alive(state), state))
        for hd, (_, acc) in zip(heads, state):
            o_ref[0, :, hd * SB_HEAD_DIM:(hd + 1) * SB_HEAD_DIM] = acc


def _sb_call(q, k, v):
    b, s, w = q.shape
    tb = ATT_BLOCK
    return pl.pallas_call(
        _sb_kernel,
        grid=(b, s // tb),
        in_specs=[pl.BlockSpec((1, tb, w), lambda bi, i: (bi, i, 0)),
                  pl.BlockSpec((1, s, w), lambda bi, i: (bi, 0, 0)),
                  pl.BlockSpec((1, s, w), lambda bi, i: (bi, 0, 0))],
        out_specs=pl.BlockSpec((1, tb, w), lambda bi, i: (bi, i, 0)),
        out_shape=jax.ShapeDtypeStruct((b, s, w), jnp.float32),
        compiler_params=pltpu.CompilerParams(
            dimension_semantics=("parallel", "arbitrary"), vmem_limit_bytes=VMEM_LIMIT),
        name="sb_attn",
    )(q, k, v)


def _mla_kernel(q_ref, k_ref, vt_ref, o_ref):
    i = pl.program_id(1)
    tb = ATT_BLOCK
    key = lax.broadcasted_iota(jnp.int32, (tb, tb), 0)
    qry = lax.broadcasted_iota(jnp.int32, (tb, tb), 1)
    allowed = (key // CHUNK) <= (qry // CHUNK)
    scale = MLA_QK_DIM ** -0.5

    def scores(hd, kb):
        qh = q_ref[0, :, hd * HEAD_PAD:(hd + 1) * HEAD_PAD]
        start = pl.multiple_of(kb * tb, tb)
        ks = k_ref[0, pl.ds(start, tb), hd * HEAD_PAD:(hd + 1) * HEAD_PAD]
        vt = vt_ref[0, kb, hd * MLA_V_DIM:(hd + 1) * MLA_V_DIM, :]
        st = lax.dot_general(ks, qh, (((1,), (1,)), ((), ())),
                             preferred_element_type=jnp.float32) * (scale * LOG2E)
        return st, vt

    outs = []
    for grp in range(MLA_HEADS // MLA_HEAD_GROUP):
        heads = range(grp * MLA_HEAD_GROUP, (grp + 1) * MLA_HEAD_GROUP)

        def walk(kb, st, masked):
            tiles = [scores(hd, kb) for hd in heads]
            mids = []
            for (s_t, vt), (m, l, acc) in zip(tiles, st):
                if masked:
                    s_t = jnp.where(allowed, s_t, -jnp.inf)
                m_new = jnp.maximum(m, jnp.max(s_t, axis=0, keepdims=True))
                alpha = jnp.exp2(m - m_new)
                p = jnp.exp2(s_t - m_new)
                l = alpha * l + jnp.sum(p, axis=0, keepdims=True)
                mids.append((m_new, l, alpha, p.astype(jnp.bfloat16), vt, acc))
            return [(m, l, alpha * acc + jnp.dot(vt, p, preferred_element_type=jnp.float32))
                    for m, l, alpha, p, vt, acc in mids]

        empty = (jnp.full((1, tb), -jnp.inf, jnp.float32), jnp.zeros((1, tb), jnp.float32),
                 jnp.zeros((MLA_V_DIM, tb), jnp.float32))
        state = walk(i, [empty] * len(heads), True)
        state = lax.fori_loop(0, i, lambda kb, st: walk(kb, st, False), state)
        outs += [acc / l for _, l, acc in state]
    o_ref[0] = jnp.concatenate(outs, axis=0).T


def _mla_call(q, k, vt):
    b, s, wq = q.shape
    _, nblk, wv, tb = vt.shape
    return pl.pallas_call(
        _mla_kernel,
        grid=(b, s // tb),
        in_specs=[pl.BlockSpec((1, tb, wq), lambda bi, i: (bi, i, 0)),
                  pl.BlockSpec((1, s, wq), lambda bi, i: (bi, 0, 0)),
                  pl.BlockSpec((1, nblk, wv, tb), lambda bi, i: (bi, 0, 0, 0))],
        out_specs=pl.BlockSpec((1, tb, wv), lambda bi, i: (bi, i, 0)),
        out_shape=jax.ShapeDtypeStruct((b, s, wv), jnp.float32),
        compiler_params=pltpu.CompilerParams(
            dimension_semantics=("parallel", "arbitrary"), vmem_limit_bytes=VMEM_LIMIT),
        name="mla_attn",
    )(q, k, vt)


def _topk_rows(a, k, ids=None):
    r, t = a.shape
    rows = lax.broadcasted_iota(jnp.int32, (r, t), 0) if ids is None else ids
    krow = lax.broadcasted_iota(jnp.int32, (k, t), 0)
    big = jnp.iinfo(jnp.int32).max

    def body(it, c):
        a, vals, idxs = c
        m = jnp.max(a, axis=0, keepdims=True)
        am = jnp.min(jnp.where(a == m, rows, big), axis=0, keepdims=True)
        a = jnp.where(rows == am, -jnp.inf, a)
        vals = jnp.where(krow == it, m, vals)
        idxs = jnp.where(krow == it, am, idxs)
        return a, vals, idxs

    _, vals, idxs = lax.fori_loop(
        0, k, body, (a, jnp.zeros((k, t), jnp.float32), jnp.zeros((k, t), jnp.int32)))
    return vals, idxs


def _select_rows(sel, table):
    out = jnp.zeros(sel.shape, table.dtype)
    for r in range(table.shape[0]):
        out = jnp.where(sel == r, table[r:r + 1, :], out)
    return out


def _candidate_tiles(k):
    tiles = [("row", 0, s) for s in range(0, k, 8)] + [("col", 0, s) for s in range(0, k, 8)]
    f = 1
    while (f + 1) * (f + 1) <= k:
        tiles += [("row", f, 0), ("col", f, 0)]
        f += 1
    seen, used, ids = set(), [], []
    for kind, fixed, start in tiles:
        tile_ids = []
        for r in range(8):
            pair = (fixed, start + r) if kind == "row" else (start + r, fixed)
            ok = (pair[0] + 1) * (pair[1] + 1) <= k and pair not in seen
            seen.add(pair)
            tile_ids.append(pair[0] * k + pair[1] if ok else -1)
        if max(tile_ids) >= 0:
            used.append((kind, fixed, start))
            ids += tile_ids
    want = sum(1 for a in range(k) for b in range(k) if (a + 1) * (b + 1) <= k)
    assert sum(i >= 0 for i in ids) == want, "candidate tiles must cover the whole staircase"
    return used, ids


_CAND_TILES, _CAND_IDS = _candidate_tiles(PEER_TOPK)


def _route_kernel(x_ref, sb_ref, mla_ref, sbn_ref, mlan_ref, wo_ref, fn_ref, wq_ref,
                  keys_ref, x2_ref, hn_ref, score_ref):
    bf = jnp.bfloat16
    sbn = _rms(sb_ref[...], sbn_ref[...]).astype(bf)
    mlan = _rms(mla_ref[...], mlan_ref[...]).astype(bf)
    attn = (jnp.dot(sbn, wo_ref[0:SB_WIDTH, :], preferred_element_type=jnp.float32)
            + jnp.dot(mlan, wo_ref[SB_WIDTH:SB_WIDTH + MLA_WIDTH, :],
                      preferred_element_type=jnp.float32))
    x2 = x_ref[...] + attn
    x2_ref[...] = x2
    hn = _rms(x2, fn_ref[...])
    hn_ref[...] = hn
    qp = jnp.dot(hn.astype(bf), wq_ref[...], preferred_element_type=jnp.float32)
    for hc in range(2 * PEER_HEADS):
        qhc = qp[:, hc * PEER_HALF_DIM:(hc + 1) * PEER_HALF_DIM].astype(bf)
        score_ref[0, hc] = lax.dot_general(keys_ref[hc], qhc, (((1,), (1,)), ((), ())),
                                           preferred_element_type=jnp.float32)


def _route_call(x2d, sb, mla, sbn, mlan, wo, fn, wq, keys):
    n = x2d.shape[0]
    t = EXPERT_TOKENS
    full = lambda a: pl.BlockSpec(a.shape, lambda i: (0,) * a.ndim)
    tok = lambda w: pl.BlockSpec((t, w), lambda i: (i, 0))
    return pl.pallas_call(
        _route_kernel,
        grid=(n // t,),
        in_specs=[tok(D_MODEL), tok(SB_WIDTH), tok(MLA_WIDTH), full(sbn), full(mlan),
                  full(wo), full(fn), full(wq), full(keys)],
        out_specs=[tok(D_MODEL), tok(D_MODEL),
                   pl.BlockSpec((1, 2 * PEER_HEADS, PEER_N_KEYS, t), lambda i: (i, 0, 0, 0))],
        out_shape=[jax.ShapeDtypeStruct((n, D_MODEL), jnp.float32),
                   jax.ShapeDtypeStruct((n, D_MODEL), jnp.float32),
                   jax.ShapeDtypeStruct((n // t, 2 * PEER_HEADS, PEER_N_KEYS, t), jnp.float32)],
        compiler_params=pltpu.CompilerParams(
            dimension_semantics=("parallel",), vmem_limit_bytes=VMEM_LIMIT),
        name="route",
    )(x2d, sb, mla, sbn, mlan, wo, fn, wq, keys)


def _expert_kernel(uv_hbm, hn_ref, x2_ref, sc0_ref, scn_ref, ids_ref, o_ref,
                   idx_vm, idx_smem, gate_vm, buf_even, buf_odd, idx_sem, row_sem):
    step = pl.program_id(0)
    nsteps = pl.num_programs(0)
    tb = EXPERT_TOKENS
    nu = GATHER_UNROLL
    ahead_b = GATHER_AHEAD
    per_buf = GATHER_SLOTS // 2
    bufs = (buf_even, buf_odd)
    head_bunches = tb // nu // PEER_HEADS
    assert ahead_b % 2 == 1 and GATHER_SLOTS == (ahead_b + 1) * nu
    assert (tb // nu) % (2 * (per_buf // nu)) == 0
    assert head_bunches % 2 == 0 and head_bunches >= 7
    cur = step % 2
    has_next = step + 1 < nsteps
    k = PEER_TOPK

    def candidates(s0, s1):
        pieces = []
        for kind, fixed, start in _CAND_TILES:
            if kind == "row":
                pieces.append(s0[fixed:fixed + 1, :] + s1[start:start + 8, :])
            else:
                pieces.append(s0[start:start + 8, :] + s1[fixed:fixed + 1, :])
        return jnp.where(ids_ref[...] >= 0, jnp.concatenate(pieces, axis=0), -jnp.inf)

    def finish(hd, slot, best, pos, i0, i1):
        e0 = _select_rows(pos // k, i0)
        e1 = _select_rows(pos % k, i1)
        p = jnp.exp(best - best[0:1, :])
        gate = p / jnp.sum(p, axis=0, keepdims=True)
        rows = pl.ds(hd * k if isinstance(hd, int) else pl.multiple_of(hd * k, k), k)
        idx_vm[rows, :] = e0 * PEER_N_KEYS + e1
        for tile in range(tb // 128):
            gate_vm[slot, tile, rows, :] = gate[:, tile * 128:(tile + 1) * 128]

    def topk_steps(state, its, ids=None):
        a, vals, idxs = state
        rows = lax.broadcasted_iota(jnp.int32, a.shape, 0) if ids is None else ids
        krow = lax.broadcasted_iota(jnp.int32, vals.shape, 0)
        big = jnp.iinfo(jnp.int32).max
        for it in its:
            m = jnp.max(a, axis=0, keepdims=True)
            am = jnp.min(jnp.where(a == m, rows, big), axis=0, keepdims=True)
            a = jnp.where(rows == am, -jnp.inf, a)
            vals = jnp.where(krow == it, m, vals)
            idxs = jnp.where(krow == it, am, idxs)
        return a, vals, idxs

    def fresh(a):
        return a, jnp.zeros((k, tb), jnp.float32), jnp.zeros((k, tb), jnp.int32)

    def idx_copy():
        return pltpu.make_async_copy(idx_vm, idx_smem, idx_sem)

    def place(b, j):
        par, half = b
        return par, (half % (per_buf // nu)) * nu + j

    def start_token(t, par, slot):
        base = slot * SLOT_ROWS
        for p in range(PEER_PICKS):
            src = pl.multiple_of(idx_smem[p, t] * SLAB_ROWS, SLAB_ROWS)
            pltpu.make_async_copy(
                uv_hbm.at[pl.ds(src, SLAB_ROWS), :],
                bufs[par].at[pl.ds(base + p * SLAB_PITCH, SLAB_ROWS), :],
                row_sem.at[par * per_buf + slot]).start(priority=p % 2)

    def wait_token(par, slot):
        rows = PEER_PICKS * SLAB_ROWS
        pltpu.make_async_copy(uv_hbm.at[pl.ds(0, rows), :],
                              bufs[par].at[pl.ds(slot * SLOT_ROWS, rows), :],
                              row_sem.at[par * per_buf + slot]).wait()

    @pl.when(step == 0)
    def _():
        def head(hd, _):
            s0, i0 = _topk_rows(sc0_ref[0, 2 * hd], k)
            s1, i1 = _topk_rows(sc0_ref[0, 2 * hd + 1], k)
            best, pos = _topk_rows(candidates(s0, s1), k, ids_ref[...])
            finish(hd, 0, best, pos, i0, i1)
            return 0

        lax.fori_loop(0, PEER_HEADS, head, 0)
        idx_copy().start()

    idx_copy().wait()
    for b in range(ahead_b):
        for j in range(nu):
            start_token(b * nu + j, *place((b % 2, b // 2), j))

    lane = lax.broadcasted_iota(jnp.int32, (PEER_PICKS, 128), 1)
    groups = PEER_PICKS // 8
    tiles = D_MODEL // 128

    def compute_token(t, par, slot):
        buf = bufs[par]
        base = slot * SLOT_ROWS
        xrow = hn_ref[pl.ds(t, 1), :]
        pres = []
        for g in range(groups):
            acc = None
            for c in range(tiles):
                u = buf[pl.ds(base + g * 8 * SLAB_PITCH + c, 8, stride=SLAB_PITCH), :]
                term = u * xrow[:, c * 128:(c + 1) * 128]
                acc = term if acc is None else acc + term
            pres.append(jnp.sum(acc, axis=1, keepdims=True))
        pre = jnp.concatenate(pres, axis=0)
        gate = jnp.sum(jnp.where(lane == t % 128, gate_vm[cur, t // 128], 0.0),
                       axis=1, keepdims=True)
        coef = gate * (0.5 * pre * (1.0 + lax.erf(pre * (2.0 ** -0.5))))
        outs = []
        for c in range(tiles):
            acc = None
            for g in range(groups):
                v = buf[pl.ds(base + g * 8 * SLAB_PITCH + tiles + c, 8, stride=SLAB_PITCH), :]
                term = coef[g * 8:(g + 1) * 8, :] * v
                acc = term if acc is None else acc + term
            outs.append(jnp.sum(acc, axis=0, keepdims=True))
        out = jnp.concatenate(outs, axis=1)
        o_ref[pl.ds(t, 1), :] = x2_ref[pl.ds(t, 1), :] + out

    def bunch(par, half, refill, search=None):
        first = (2 * half + par) * nu
        for j in range(nu):
            wait_token(*place((par, half), j))
        found = search() if search is not None else None
        for j in range(nu):
            compute_token(first + j, *place((par, half), j))

        if refill:
            later = (par + ahead_b) // 2 + half
            for j in range(nu):
                start_token(first + ahead_b * nu + j, *place((1 - par, later), j))
        return found

    half_its = (range(0, k // 2), range(k // 2, k))

    def head_body(hd, last):
        def run(j, search=None):
            return bunch(j % 2, hd * (head_bunches // 2) + j // 2,
                         not last or j < head_bunches - ahead_b, search)

        st0 = fresh(scn_ref[0, 2 * hd])
        st1 = fresh(scn_ref[0, 2 * hd + 1])
        st0 = run(0, lambda: topk_steps(st0, half_its[0]))
        st0 = run(1, lambda: topk_steps(st0, half_its[1]))
        st1 = run(2, lambda: topk_steps(st1, half_its[0]))
        st1 = run(3, lambda: topk_steps(st1, half_its[1]))
        ids = ids_ref[...]
        st2 = fresh(candidates(st0[1], st1[1]))
        st2 = run(4, lambda: topk_steps(st2, half_its[0], ids))
        st2 = run(5, lambda: topk_steps(st2, half_its[1], ids))
        run(6, lambda: finish(hd, 1 - cur, st2[1], st2[2], st0[2], st1[2]))
        for j in range(7, head_bunches):
            run(j)
        return 0

    lax.fori_loop(0, PEER_HEADS - 1, lambda hd, c: head_body(hd, False), 0)
    head_body(PEER_HEADS - 1, True)

    @pl.when(has_next)
    def _():
        idx_copy().start()


def _expert_call(uv, hn, x2, scores):
    n = hn.shape[0]
    tb = EXPERT_TOKENS
    nblk = n // tb
    tok = pl.BlockSpec((tb, D_MODEL), lambda i: (i, 0))
    sc_shape = (1,) + scores.shape[1:]
    ids = jnp.broadcast_to(jnp.asarray(_CAND_IDS, jnp.int32)[:, None], (len(_CAND_IDS), tb))
    return pl.pallas_call(
        _expert_kernel,
        grid=(nblk,),
        in_specs=[pl.BlockSpec(memory_space=pl.ANY),
                  tok, tok,
                  pl.BlockSpec(sc_shape, lambda i: (0, 0, 0, 0)),
                  pl.BlockSpec(sc_shape, lambda i: (jnp.minimum(i + 1, nblk - 1), 0, 0, 0)),
                  pl.BlockSpec(ids.shape, lambda i: (0, 0))],
        out_specs=tok,
        out_shape=jax.ShapeDtypeStruct((n, D_MODEL), jnp.float32),
        scratch_shapes=[pltpu.VMEM((PEER_PICKS, tb), jnp.int32),
                        pltpu.SMEM((PEER_PICKS, tb), jnp.int32),
                        pltpu.VMEM((2, tb // 128, PEER_PICKS, 128), jnp.float32),
                        pltpu.VMEM((GATHER_SLOTS // 2 * SLOT_ROWS, 128), jnp.float32),
                        pltpu.VMEM((GATHER_SLOTS // 2 * SLOT_ROWS, 128), jnp.float32),
                        pltpu.SemaphoreType.DMA,
                        pltpu.SemaphoreType.DMA((GATHER_SLOTS,))],
        compiler_params=pltpu.CompilerParams(
            dimension_semantics=("arbitrary",), vmem_limit_bytes=VMEM_LIMIT),
        name="experts",
    )(uv, hn, x2, scores, scores, ids)


def _pad_heads(w, head_dim):
    k = w.shape[0]
    w = w.reshape(k, -1, head_dim)
    w = jnp.pad(w, ((0, 0), (0, 0), (0, HEAD_PAD - head_dim)))
    return w.reshape(k, -1)


def _layer(x2d, pos, b, s, an, w_in, cqn, w_uq, ckvn, w_ukv, qn, kn, sbn, mlan, w_o, fn,
           w_pq, sub_keys, pu, pv):
    bf = jnp.bfloat16
    row = lambda a: a.reshape(1, -1)
    kpe_cols = jnp.pad(w_in[:, _C_KPE:_C_KPE + MLA_ROPE_DIM],
                       ((0, 0), (MLA_NOPE_DIM, HEAD_PAD - MLA_QK_DIM)))
    win = jnp.concatenate([w_in[:, :_C_KPE], kpe_cols], axis=1).astype(bf)
    wuq = _pad_heads(w_uq, MLA_QK_DIM).astype(bf)
    w_ukv3 = w_ukv.reshape(MLA_KV_RANK, MLA_HEADS, MLA_NOPE_DIM + MLA_V_DIM)
    wuk = _pad_heads(w_ukv3[:, :, :MLA_NOPE_DIM].reshape(MLA_KV_RANK, -1), MLA_NOPE_DIM).astype(bf)
    wuv = w_ukv3[:, :, MLA_NOPE_DIM:].reshape(MLA_KV_RANK, -1).astype(bf)
    pad_gain = lambda g: jnp.pad(g, (0, HEAD_PAD - MLA_QK_DIM)).reshape(1, HEAD_PAD)
    half = MLA_ROPE_DIM // 2
    inv_freq = 1.0 / (ROPE_THETA ** (jnp.arange(half, dtype=jnp.float32) * (2.0 / MLA_ROPE_DIM)))
    freq = jnp.concatenate([jnp.zeros((MLA_NOPE_DIM,), jnp.float32), inv_freq, inv_freq,
                            jnp.zeros((HEAD_PAD - MLA_QK_DIM,), jnp.float32)]).reshape(1, HEAD_PAD)

    sbq, sbk, sbv, q, k, v = _prep_call(
        x2d, pos, row(an), win, row(cqn), wuq, row(ckvn), wuk, wuv,
        pad_gain(qn), pad_gain(kn), freq)

    r3 = lambda a: a.reshape(b, s, a.shape[-1])
    sb = _sb_call(r3(sbq), r3(sbk), r3(sbv)).reshape(b * s, SB_WIDTH)
    vt = v.reshape(b, s // ATT_BLOCK, MLA_WIDTH, ATT_BLOCK)
    mla = _mla_call(r3(q), r3(k), vt).reshape(b * s, MLA_WIDTH)

    keys = sub_keys.transpose(1, 0, 2, 3).reshape(2 * PEER_HEADS, PEER_N_KEYS, PEER_HALF_DIM).astype(bf)
    x2, hn, scores = _route_call(
        x2d, sb, mla, row(sbn), row(mlan), w_o.astype(bf), row(fn), w_pq.astype(bf), keys)

    slab = lambda w: w.reshape(PEER_N_EXPERTS, D_MODEL // 128, 128)
    uv = jnp.concatenate([slab(pu), slab(pv)], axis=1).reshape(PEER_N_EXPERTS * SLAB_ROWS, 128)
    return _expert_call(uv, hn, x2, scores)


def kernel(x, positions, attn_norm, w_in, cq_norm, w_uq, ckv_norm, w_ukv, q_norm, k_norm,
           sb_out_norm, mla_out_norm, w_o, ffn_norm, peer_w_q, peer_sub_keys, peer_u, peer_v):
    b, s, d = x.shape
    x2d = x.reshape(b * s, d)
    pos = positions.reshape(b * s, 1)
    for l in range(attn_norm.shape[0]):
        x2d = _layer(x2d, pos, b, s, attn_norm[l], w_in[l], cq_norm[l], w_uq[l], ckv_norm[l],
                     w_ukv[l], q_norm[l], k_norm[l], sb_out_norm[l], mla_out_norm[l], w_o[l],
                     ffn_norm[l], peer_w_q[l], peer_sub_keys[l], peer_u[l], peer_v[l])
    return x2d.reshape(b, s, d)
```

```python
import functools
import math

import jax
import jax.numpy as jnp
from jax import lax
from jax.experimental import pallas as pl
from jax.experimental.pallas import tpu as pltpu

D_MODEL = 1024
CHUNK = 64
EPS = 1e-6

SB_HEADS = 8
SB_HEAD_DIM = 64
SB_WIDTH = SB_HEADS * SB_HEAD_DIM

MLA_HEADS = 8
MLA_NOPE_DIM = 64
MLA_ROPE_DIM = 32
MLA_QK_DIM = MLA_NOPE_DIM + MLA_ROPE_DIM
MLA_V_DIM = 64
MLA_Q_RANK = 384
MLA_KV_RANK = 256
MLA_WIDTH = MLA_HEADS * MLA_V_DIM
ROPE_THETA = 10000.0
HEAD_PAD = 128
MLA_PAD_WIDTH = MLA_HEADS * HEAD_PAD

PEER_HEADS = 8
PEER_N_KEYS = 128
PEER_N_EXPERTS = PEER_N_KEYS * PEER_N_KEYS
PEER_KEY_DIM = 256
PEER_HALF_DIM = PEER_KEY_DIM // 2
PEER_TOPK = 16
PEER_PICKS = PEER_HEADS * PEER_TOPK

_C_SBQ = 0
_C_SBK = SB_WIDTH
_C_SBV = 2 * SB_WIDTH
_C_CQ = 3 * SB_WIDTH
_C_CKV = _C_CQ + MLA_Q_RANK
_C_KPE = _C_CKV + MLA_KV_RANK
IN_PAD_WIDTH = _C_KPE + HEAD_PAD

VMEM_LIMIT = 56 * 1024 * 1024

PREP_TOKENS = 512
ATT_BLOCK = 256
ATT_HEAD_GROUP = 8
MLA_HEAD_GROUP = 8
EXPERT_TOKENS = 256
GATHER_UNROLL = 4
GATHER_AHEAD = 3
GATHER_SLOTS = (GATHER_AHEAD + 1) * GATHER_UNROLL
SLAB_ROWS = 2 * D_MODEL // 128
SLAB_PITCH = SLAB_ROWS + 1
SLOT_ROWS = PEER_PICKS * SLAB_PITCH

LOG2E = math.log2(math.e)

SB_DEAD_LOG = -105.0


def _rms(x, g):
    ms = jnp.mean(x * x, axis=-1, keepdims=True)
    return x * lax.rsqrt(ms + EPS) * g


def _prep_kernel(x_ref, pos_ref, an_ref, win_ref, cqn_ref, wuq_ref, ckvn_ref,
                 wuk_ref, wuv_ref, qn_ref, kn_ref, freq_ref,
                 sbq_ref, sbk_ref, sbv_ref, q_ref, k_ref, vt_ref):
    h = _rms(x_ref[...], an_ref[...]).astype(jnp.bfloat16)
    proj = jnp.dot(h, win_ref[...], preferred_element_type=jnp.float32)
    sbq_ref[...] = (proj[:, _C_SBQ:_C_SBQ + SB_WIDTH] * (SB_HEAD_DIM ** -0.5)).astype(jnp.bfloat16)
    sbk_ref[...] = proj[:, _C_SBK:_C_SBK + SB_WIDTH].astype(jnp.bfloat16)
    sbv_ref[...] = proj[:, _C_SBV:_C_SBV + SB_WIDTH].astype(jnp.bfloat16)

    cq = _rms(proj[:, _C_CQ:_C_CQ + MLA_Q_RANK], cqn_ref[...]).astype(jnp.bfloat16)
    ckv = _rms(proj[:, _C_CKV:_C_CKV + MLA_KV_RANK], ckvn_ref[...]).astype(jnp.bfloat16)
    kpe = proj[:, _C_KPE:_C_KPE + HEAD_PAD]
    qf = jnp.dot(cq, wuq_ref[...], preferred_element_type=jnp.float32)
    kf = jnp.dot(ckv, wuk_ref[...], preferred_element_type=jnp.float32)
    vf = jnp.dot(ckv, wuv_ref[...], preferred_element_type=jnp.float32)
    for blk in range(vt_ref.shape[0]):
        vt_ref[blk] = vf[blk * ATT_BLOCK:(blk + 1) * ATT_BLOCK, :].T.astype(jnp.bfloat16)

    ang = pos_ref[...].astype(jnp.float32) * freq_ref[...]
    cos = jnp.cos(ang)
    sin = jnp.sin(ang)
    lane = lax.broadcasted_iota(jnp.int32, ang.shape, 1)
    half = MLA_ROPE_DIM // 2
    lo = (lane >= MLA_NOPE_DIM) & (lane < MLA_NOPE_DIM + half)
    hi = (lane >= MLA_NOPE_DIM + half) & (lane < MLA_QK_DIM)
    sin_lo = jnp.where(lo, -sin, 0.0)
    sin_hi = jnp.where(hi, sin, 0.0)

    def norm_rope(xh, g):
        ms = jnp.sum(xh * xh, axis=-1, keepdims=True) * (1.0 / MLA_QK_DIM)
        xn = xh * lax.rsqrt(ms + EPS) * g
        return (xn * cos + pltpu.roll(xn, HEAD_PAD - half, 1) * sin_lo
                + pltpu.roll(xn, half, 1) * sin_hi)

    for hd in range(MLA_HEADS):
        sl = slice(hd * HEAD_PAD, (hd + 1) * HEAD_PAD)
        q_ref[:, sl] = norm_rope(qf[:, sl], qn_ref[...]).astype(jnp.bfloat16)
        k_ref[:, sl] = norm_rope(kf[:, sl] + kpe, kn_ref[...]).astype(jnp.bfloat16)


def _prep_call(x2d, pos, an, win, cqn, wuq, ckvn, wuk, wuv, qn, kn, freq):
    n = x2d.shape[0]
    t = PREP_TOKENS
    full = lambda a: pl.BlockSpec(a.shape, lambda i: (0,) * a.ndim)
    tok = lambda w: pl.BlockSpec((t, w), lambda i: (i, 0))
    bf = jnp.bfloat16
    return pl.pallas_call(
        _prep_kernel,
        grid=(n // t,),
        in_specs=[tok(D_MODEL), tok(1), full(an), full(win), full(cqn), full(wuq),
                  full(ckvn), full(wuk), full(wuv), full(qn), full(kn), full(freq)],
        out_specs=[tok(SB_WIDTH), tok(SB_WIDTH), tok(SB_WIDTH),
                   tok(MLA_PAD_WIDTH), tok(MLA_PAD_WIDTH),
                   pl.BlockSpec((t // ATT_BLOCK, MLA_WIDTH, ATT_BLOCK), lambda i: (i, 0, 0))],
        out_shape=[jax.ShapeDtypeStruct((n, SB_WIDTH), bf)] * 3
        + [jax.ShapeDtypeStruct((n, MLA_PAD_WIDTH), bf)] * 2
        + [jax.ShapeDtypeStruct((n // ATT_BLOCK, MLA_WIDTH, ATT_BLOCK), bf)],
        compiler_params=pltpu.CompilerParams(
            dimension_semantics=("parallel",), vmem_limit_bytes=VMEM_LIMIT),
        name="prep",
    )(x2d, pos, an, win, cqn, wuq, ckvn, wuk, wuv, qn, kn, freq)


def _sb_kernel(q_ref, k_ref, v_ref, o_ref):
    i = pl.program_id(1)
    tb = ATT_BLOCK
    row = lax.broadcasted_iota(jnp.int32, (tb, tb), 0)
    col = lax.broadcasted_iota(jnp.int32, (tb, tb), 1)
    strict = col < row
    upper = jnp.where(row > col, 1.0, 0.0).astype(jnp.bfloat16)

    for grp in range(SB_HEADS // ATT_HEAD_GROUP):
        heads = range(grp * ATT_HEAD_GROUP, (grp + 1) * ATT_HEAD_GROUP)

        def walk(kb, st, masked):
            start = pl.multiple_of(kb * tb, tb)
            heads_sl = [slice(hd * SB_HEAD_DIM, (hd + 1) * SB_HEAD_DIM) for hd in heads]
            zs = [lax.dot_general(q_ref[0, :, sl], k_ref[0, pl.ds(start, tb), sl],
                                  (((1,), (1,)), ((), ())), preferred_element_type=jnp.float32)
                  for sl in heads_sl]
            mids = []
            for z in zs:
                lf_all = -(jnp.maximum(z, 0.0) + jnp.log(1.0 + jnp.exp(-jnp.abs(z))))
                lf = jnp.where(strict, lf_all, 0.0) if masked else lf_all
                lf_hi = lf.astype(jnp.bfloat16)
                lf_lo = (lf - lf_hi.astype(jnp.float32)).astype(jnp.bfloat16)
                mids.append((z + lf_all, lf_hi, lf_lo, jnp.sum(lf, axis=-1, keepdims=True)))
            laters = [jnp.dot(hi, upper, preferred_element_type=jnp.float32)
                      + jnp.dot(lo, upper, preferred_element_type=jnp.float32)
                      for _, hi, lo, _ in mids]
            weights = []
            for (ls, _, _, _), later, (carry, _) in zip(mids, laters, st):
                a = jnp.exp(ls + later + carry)
                if masked:
                    a = jnp.where(strict, a, 0.0)
                weights.append(a.astype(jnp.bfloat16))
            return [(carry + rs,
                     acc + jnp.dot(a, v_ref[0, pl.ds(start, tb), sl],
                                   preferred_element_type=jnp.float32))
                    for a, sl, (_, _, _, rs), (carry, acc) in zip(weights, heads_sl, mids, st)]

        carry0 = jnp.zeros((tb, 1), jnp.float32)
        acc0 = jnp.zeros((tb, SB_HEAD_DIM), jnp.float32)
        state = walk(i, [(carry0, acc0)] * len(heads), True)

        def alive(st):
            top = st[0][0]
            for c, _ in st[1:]:
                top = jnp.maximum(top, c)
            return jnp.max(top) > SB_DEAD_LOG

        def cond(s):
            return (s[0] < i) & s[1]

        def body(s):
            j, _, st = s
            st = walk(i - 1 - j, st, False)
            return j + 1, alive(st), st

        _, _, state = lax.while_loop(cond, body, (0, alive(state), state))
        for hd, (_, acc) in zip(heads, state):
            o_ref[0, :, hd * SB_HEAD_DIM:(hd + 1) * SB_HEAD_DIM] = acc


def _sb_call(q, k, v):
    b, s, w = q.shape
    tb = ATT_BLOCK
    return pl.pallas_call(
        _sb_kernel,
        grid=(b, s // tb),
        in_specs=[pl.BlockSpec((1, tb, w), lambda bi, i: (bi, i, 0)),
                  pl.BlockSpec((1, s, w), lambda bi, i: (bi, 0, 0)),
                  pl.BlockSpec((1, s, w), lambda bi, i: (bi, 0, 0))],
        out_specs=pl.BlockSpec((1, tb, w), lambda bi, i: (bi, i, 0)),
        out_shape=jax.ShapeDtypeStruct((b, s, w), jnp.float32),
        compiler_params=pltpu.CompilerParams(
            dimension_semantics=("parallel", "arbitrary"), vmem_limit_bytes=VMEM_LIMIT),
        name="sb_attn",
    )(q, k, v)


def _mla_kernel(q_ref, k_ref, vt_ref, o_ref):
    i = pl.program_id(1)
    tb = ATT_BLOCK
    key = lax.broadcasted_iota(jnp.int32, (tb, tb), 0)
    qry = lax.broadcasted_iota(jnp.int32, (tb, tb), 1)
    allowed = (key // CHUNK) <= (qry // CHUNK)
    scale = MLA_QK_DIM ** -0.5

    def scores(hd, kb):
        qh = q_ref[0, :, hd * HEAD_PAD:(hd + 1) * HEAD_PAD]
        start = pl.multiple_of(kb * tb, tb)
        ks = k_ref[0, pl.ds(start, tb), hd * HEAD_PAD:(hd + 1) * HEAD_PAD]
        vt = vt_ref[0, kb, hd * MLA_V_DIM:(hd + 1) * MLA_V_DIM, :]
        st = lax.dot_general(ks, qh, (((1,), (1,)), ((), ())),
                             preferred_element_type=jnp.float32) * (scale * LOG2E)
        return st, vt

    outs = []
    for grp in range(MLA_HEADS // MLA_HEAD_GROUP):
        heads = range(grp * MLA_HEAD_GROUP, (grp + 1) * MLA_HEAD_GROUP)

        def walk(kb, st, masked):
            tiles = [scores(hd, kb) for hd in heads]
            mids = []
            for (s_t, vt), (m, l, acc) in zip(tiles, st):
                if masked:
                    s_t = jnp.where(allowed, s_t, -jnp.inf)
                m_new = jnp.maximum(m, jnp.max(s_t, axis=0, keepdims=True))
                alpha = jnp.exp2(m - m_new)
                p = jnp.exp2(s_t - m_new)
                l = alpha * l + jnp.sum(p, axis=0, keepdims=True)
                mids.append((m_new, l, alpha, p.astype(jnp.bfloat16), vt, acc))
            return [(m, l, alpha * acc + jnp.dot(vt, p, preferred_element_type=jnp.float32))
                    for m, l, alpha, p, vt, acc in mids]

        empty = (jnp.full((1, tb), -jnp.inf, jnp.float32), jnp.zeros((1, tb), jnp.float32),
                 jnp.zeros((MLA_V_DIM, tb), jnp.float32))
        state = walk(i, [empty] * len(heads), True)
        state = lax.fori_loop(0, i, lambda kb, st: walk(kb, st, False), state)
        outs += [acc / l for _, l, acc in state]
    o_ref[0] = jnp.concatenate(outs, axis=0).T


def _mla_call(q, k, vt):
    b, s, wq = q.shape
    _, nblk, wv, tb = vt.shape
    return pl.pallas_call(
        _mla_kernel,
        grid=(b, s // tb),
        in_specs=[pl.BlockSpec((1, tb, wq), lambda bi, i: (bi, i, 0)),
                  pl.BlockSpec((1, s, wq), lambda bi, i: (bi, 0, 0)),
                  pl.BlockSpec((1, nblk, wv, tb), lambda bi, i: (bi, 0, 0, 0))],
        out_specs=pl.BlockSpec((1, tb, wv), lambda bi, i: (bi, i, 0)),
        out_shape=jax.ShapeDtypeStruct((b, s, wv), jnp.float32),
        compiler_params=pltpu.CompilerParams(
            dimension_semantics=("parallel", "arbitrary"), vmem_limit_bytes=VMEM_LIMIT),
        name="mla_attn",
    )(q, k, vt)


def _topk_rows(a, k, ids=None):
    r, t = a.shape
    rows = lax.broadcasted_iota(jnp.int32, (r, t), 0) if ids is None else ids
    krow = lax.broadcasted_iota(jnp.int32, (k, t), 0)
    big = jnp.iinfo(jnp.int32).max

    def body(it, c):
        a, vals, idxs = c
        m = jnp.max(a, axis=0, keepdims=True)
        am = jnp.min(jnp.where(a == m, rows, big), axis=0, keepdims=True)
        a = jnp.where(rows == am, -jnp.inf, a)
        vals = jnp.where(krow == it, m, vals)
        idxs = jnp.where(krow == it, am, idxs)
        return a, vals, idxs

    _, vals, idxs = lax.fori_loop(
        0, k, body, (a, jnp.zeros((k, t), jnp.float32), jnp.zeros((k, t), jnp.int32)))
    return vals, idxs


def _select_rows(sel, table):
    out = jnp.zeros(sel.shape, table.dtype)
    for r in range(table.shape[0]):
        out = jnp.where(sel == r, table[r:r + 1, :], out)
    return out


def _candidate_tiles(k):
    tiles = [("row", 0, s) for s in range(0, k, 8)] + [("col", 0, s) for s in range(0, k, 8)]
    f = 1
    while (f + 1) * (f + 1) <= k:
        tiles += [("row", f, 0), ("col", f, 0)]
        f += 1
    seen, used, ids = set(), [], []
    for kind, fixed, start in tiles:
        tile_ids = []
        for r in range(8):
            pair = (fixed, start + r) if kind == "row" else (start + r, fixed)
            ok = (pair[0] + 1) * (pair[1] + 1) <= k and pair not in seen
            seen.add(pair)
            tile_ids.append(pair[0] * k + pair[1] if ok else -1)
        if max(tile_ids) >= 0:
            used.append((kind, fixed, start))
            ids += tile_ids
    want = sum(1 for a in range(k) for b in range(k) if (a + 1) * (b + 1) <= k)
    assert sum(i >= 0 for i in ids) == want, "candidate tiles must cover the whole staircase"
    return used, ids


_CAND_TILES, _CAND_IDS = _candidate_tiles(PEER_TOPK)


def _route_kernel(x_ref, sb_ref, mla_ref, sbn_ref, mlan_ref, wo_ref, fn_ref, wq_ref,
                  keys_ref, x2_ref, hn_ref, score_ref):
    bf = jnp.bfloat16
    sbn = _rms(sb_ref[...], sbn_ref[...]).astype(bf)
    mlan = _rms(mla_ref[...], mlan_ref[...]).astype(bf)
    attn = (jnp.dot(sbn, wo_ref[0:SB_WIDTH, :], preferred_element_type=jnp.float32)
            + jnp.dot(mlan, wo_ref[SB_WIDTH:SB_WIDTH + MLA_WIDTH, :],
                      preferred_element_type=jnp.float32))
    x2 = x_ref[...] + attn
    x2_ref[...] = x2
    hn = _rms(x2, fn_ref[...])
    hn_ref[...] = hn
    qp = jnp.dot(hn.astype(bf), wq_ref[...], preferred_element_type=jnp.float32)
    for hc in range(2 * PEER_HEADS):
        qhc = qp[:, hc * PEER_HALF_DIM:(hc + 1) * PEER_HALF_DIM].astype(bf)
        score_ref[0, hc] = lax.dot_general(keys_ref[hc], qhc, (((1,), (1,)), ((), ())),
                                           preferred_element_type=jnp.float32)


def _route_call(x2d, sb, mla, sbn, mlan, wo, fn, wq, keys):
    n = x2d.shape[0]
    t = EXPERT_TOKENS
    full = lambda a: pl.BlockSpec(a.shape, lambda i: (0,) * a.ndim)
    tok = lambda w: pl.BlockSpec((t, w), lambda i: (i, 0))
    return pl.pallas_call(
        _route_kernel,
        grid=(n // t,),
        in_specs=[tok(D_MODEL), tok(SB_WIDTH), tok(MLA_WIDTH), full(sbn), full(mlan),
                  full(wo), full(fn), full(wq), full(keys)],
        out_specs=[tok(D_MODEL), tok(D_MODEL),
                   pl.BlockSpec((1, 2 * PEER_HEADS, PEER_N_KEYS, t), lambda i: (i, 0, 0, 0))],
        out_shape=[jax.ShapeDtypeStruct((n, D_MODEL), jnp.float32),
                   jax.ShapeDtypeStruct((n, D_MODEL), jnp.float32),
                   jax.ShapeDtypeStruct((n // t, 2 * PEER_HEADS, PEER_N_KEYS, t), jnp.float32)],
        compiler_params=pltpu.CompilerParams(
            dimension_semantics=("parallel",), vmem_limit_bytes=VMEM_LIMIT),
        name="route",
    )(x2d, sb, mla, sbn, mlan, wo, fn, wq, keys)


def _expert_kernel(uv_hbm, hn_ref, x2_ref, sc0_ref, scn_ref, ids_ref, o_ref,
                   idx_vm, idx_smem, gate_vm, buf_even, buf_odd, idx_sem, row_sem):
    step = pl.program_id(0)
    nsteps = pl.num_programs(0)
    tb = EXPERT_TOKENS
    nu = GATHER_UNROLL
    ahead_b = GATHER_AHEAD
    per_buf = GATHER_SLOTS // 2
    bufs = (buf_even, buf_odd)
    head_bunches = tb // nu // PEER_HEADS
    assert ahead_b % 2 == 1 and GATHER_SLOTS == (ahead_b + 1) * nu
    assert (tb // nu) % (2 * (per_buf // nu)) == 0
    assert head_bunches % 2 == 0 and head_bunches >= 7
    cur = step % 2
    has_next = step + 1 < nsteps
    k = PEER_TOPK

    def candidates(s0, s1):
        pieces = []
        for kind, fixed, start in _CAND_TILES:
            if kind == "row":
                pieces.append(s0[fixed:fixed + 1, :] + s1[start:start + 8, :])
            else:
                pieces.append(s0[start:start + 8, :] + s1[fixed:fixed + 1, :])
        return jnp.where(ids_ref[...] >= 0, jnp.concatenate(pieces, axis=0), -jnp.inf)

    def finish(hd, slot, best, pos, i0, i1):
        e0 = _select_rows(pos // k, i0)
        e1 = _select_rows(pos % k, i1)
        p = jnp.exp(best - best[0:1, :])
        gate = p / jnp.sum(p, axis=0, keepdims=True)
        rows = pl.ds(hd * k if isinstance(hd, int) else pl.multiple_of(hd * k, k), k)
        idx_vm[rows, :] = e0 * PEER_N_KEYS + e1
        for tile in range(tb // 128):
            gate_vm[slot, tile, rows, :] = gate[:, tile * 128:(tile + 1) * 128]

    def topk_steps(state, its, ids=None):
        a, vals, idxs = state
        rows = lax.broadcasted_iota(jnp.int32, a.shape, 0) if ids is None else ids
        krow = lax.broadcasted_iota(jnp.int32, vals.shape, 0)
        big = jnp.iinfo(jnp.int32).max
        for it in its:
            m = jnp.max(a, axis=0, keepdims=True)
            am = jnp.min(jnp.where(a == m, rows, big), axis=0, keepdims=True)
            a = jnp.where(rows == am, -jnp.inf, a)
            vals = jnp.where(krow == it, m, vals)
            idxs = jnp.where(krow == it, am, idxs)
        return a, vals, idxs

    def fresh(a):
        return a, jnp.zeros((k, tb), jnp.float32), jnp.zeros((k, tb), jnp.int32)

    def idx_copy():
        return pltpu.make_async_copy(idx_vm, idx_smem, idx_sem)

    def place(b, j):
        par, half = b
        return par, (half % (per_buf // nu)) * nu + j

    def start_token(t, par, slot):
        base = slot * SLOT_ROWS
        for p in range(PEER_PICKS):
            src = pl.multiple_of(idx_smem[p, t] * SLAB_ROWS, SLAB_ROWS)
            pltpu.make_async_copy(
                uv_hbm.at[pl.ds(src, SLAB_ROWS), :],
                bufs[par].at[pl.ds(base + p * SLAB_PITCH, SLAB_ROWS), :],
                row_sem.at[par * per_buf + slot]).start(priority=p % 2)

    def wait_token(par, slot):
        rows = PEER_PICKS * SLAB_ROWS
        pltpu.make_async_copy(uv_hbm.at[pl.ds(0, rows), :],
                              bufs[par].at[pl.ds(slot * SLOT_ROWS, rows), :],
                              row_sem.at[par * per_buf + slot]).wait()

    @pl.when(step == 0)
    def _():
        def head(hd, _):
            s0, i0 = _topk_rows(sc0_ref[0, 2 * hd], k)
            s1, i1 = _topk_rows(sc0_ref[0, 2 * hd + 1], k)
            best, pos = _topk_rows(candidates(s0, s1), k, ids_ref[...])
            finish(hd, 0, best, pos, i0, i1)
            return 0

        lax.fori_loop(0, PEER_HEADS, head, 0)
        idx_copy().start()

    idx_copy().wait()
    for b in range(ahead_b):
        for j in range(nu):
            start_token(b * nu + j, *place((b % 2, b // 2), j))

    lane = lax.broadcasted_iota(jnp.int32, (PEER_PICKS, 128), 1)
    groups = PEER_PICKS // 8
    tiles = D_MODEL // 128

    def compute_token(t, par, slot):
        buf = bufs[par]
        base = slot * SLOT_ROWS
        xrow = hn_ref[pl.ds(t, 1), :]
        pres = []
        for g in range(groups):
            acc = None
            for c in range(tiles):
                u = buf[pl.ds(base + g * 8 * SLAB_PITCH + c, 8, stride=SLAB_PITCH), :]
                term = u * xrow[:, c * 128:(c + 1) * 128]
                acc = term if acc is None else acc + term
            pres.append(jnp.sum(acc, axis=1, keepdims=True))
        pre = jnp.concatenate(pres, axis=0)
        gate = jnp.sum(jnp.where(lane == t % 128, gate_vm[cur, t // 128], 0.0),
                       axis=1, keepdims=True)
        coef = gate * (0.5 * pre * (1.0 + lax.erf(pre * (2.0 ** -0.5))))
        outs = []
        for c in range(tiles):
            acc = None
            for g in range(groups):
                v = buf[pl.ds(base + g * 8 * SLAB_PITCH + tiles + c, 8, stride=SLAB_PITCH), :]
                term = coef[g * 8:(g + 1) * 8, :] * v
                acc = term if acc is None else acc + term
            outs.append(jnp.sum(acc, axis=0, keepdims=True))
        out = jnp.concatenate(outs, axis=1)
        o_ref[pl.ds(t, 1), :] = x2_ref[pl.ds(t, 1), :] + out

    def bunch(par, half, refill, search=None):
        first = (2 * half + par) * nu
        for j in range(nu):
            wait_token(*place((par, half), j))
        found = search() if search is not None else None
        for j in range(nu):
            compute_token(first + j, *place((par, half), j))

        if refill:
            later = (par + ahead_b) // 2 + half
            for j in range(nu):
                start_token(first + ahead_b * nu + j, *place((1 - par, later), j))
        return found

    half_its = (range(0, k // 2), range(k // 2, k))

    def head_body(hd, last):
        def run(j, search=None):
            return bunch(j % 2, hd * (head_bunches // 2) + j // 2,
                         not last or j < head_bunches - ahead_b, search)

        st0 = fresh(scn_ref[0, 2 * hd])
        st1 = fresh(scn_ref[0, 2 * hd + 1])
        st0 = run(0, lambda: topk_steps(st0, half_its[0]))
        st0 = run(1, lambda: topk_steps(st0, half_its[1]))
        st1 = run(2, lambda: topk_steps(st1, half_its[0]))
        st1 = run(3, lambda: topk_steps(st1, half_its[1]))
        ids = ids_ref[...]
        st2 = fresh(candidates(st0[1], st1[1]))
        st2 = run(4, lambda: topk_steps(st2, half_its[0], ids))
        st2 = run(5, lambda: topk_steps(st2, half_its[1], ids))
        run(6, lambda: finish(hd, 1 - cur, st2[1], st2[2], st0[2], st1[2]))
        for j in range(7, head_bunches):
            run(j)
        return 0

    lax.fori_loop(0, PEER_HEADS - 1, lambda hd, c: head_body(hd, False), 0)
    head_body(PEER_HEADS - 1, True)

    @pl.when(has_next)
    def _():
        idx_copy().start()


def _expert_call(uv, hn, x2, scores):
    n = hn.shape[0]
    tb = EXPERT_TOKENS
    nblk = n // tb
    tok = pl.BlockSpec((tb, D_MODEL), lambda i: (i, 0))
    sc_shape = (1,) + scores.shape[1:]
    ids = jnp.broadcast_to(jnp.asarray(_CAND_IDS, jnp.int32)[:, None], (len(_CAND_IDS), tb))
    return pl.pallas_call(
        _expert_kernel,
        grid=(nblk,),
        in_specs=[pl.BlockSpec(memory_space=pl.ANY),
                  tok, tok,
                  pl.BlockSpec(sc_shape, lambda i: (0, 0, 0, 0)),
                  pl.BlockSpec(sc_shape, lambda i: (jnp.minimum(i + 1, nblk - 1), 0, 0, 0)),
                  pl.BlockSpec(ids.shape, lambda i: (0, 0))],
        out_specs=tok,
        out_shape=jax.ShapeDtypeStruct((n, D_MODEL), jnp.float32),
        scratch_shapes=[pltpu.VMEM((PEER_PICKS, tb), jnp.int32),
                        pltpu.SMEM((PEER_PICKS, tb), jnp.int32),
                        pltpu.VMEM((2, tb // 128, PEER_PICKS, 128), jnp.float32),
                        pltpu.VMEM((GATHER_SLOTS // 2 * SLOT_ROWS, 128), jnp.float32),
                        pltpu.VMEM((GATHER_SLOTS // 2 * SLOT_ROWS, 128), jnp.float32),
                        pltpu.SemaphoreType.DMA,
                        pltpu.SemaphoreType.DMA((GATHER_SLOTS,))],
        compiler_params=pltpu.CompilerParams(
            dimension_semantics=("arbitrary",), vmem_limit_bytes=VMEM_LIMIT),
        name="experts",
    )(uv, hn, x2, scores, scores, ids)


def _pad_heads(w, head_dim):
    k = w.shape[0]
    w = w.reshape(k, -1, head_dim)
    w = jnp.pad(w, ((0, 0), (0, 0), (0, HEAD_PAD - head_dim)))
    return w.reshape(k, -1)


def _layer(x2d, pos, b, s, an, w_in, cqn, w_uq, ckvn, w_ukv, qn, kn, sbn, mlan, w_o, fn,
           w_pq, sub_keys, pu, pv):
    bf = jnp.bfloat16
    row = lambda a: a.reshape(1, -1)
    kpe_cols = jnp.pad(w_in[:, _C_KPE:_C_KPE + MLA_ROPE_DIM],
                       ((0, 0), (MLA_NOPE_DIM, HEAD_PAD - MLA_QK_DIM)))
    win = jnp.concatenate([w_in[:, :_C_KPE], kpe_cols], axis=1).astype(bf)
    wuq = _pad_heads(w_uq, MLA_QK_DIM).astype(bf)
    w_ukv3 = w_ukv.reshape(MLA_KV_RANK, MLA_HEADS, MLA_NOPE_DIM + MLA_V_DIM)
    wuk = _pad_heads(w_ukv3[:, :, :MLA_NOPE_DIM].reshape(MLA_KV_RANK, -1), MLA_NOPE_DIM).astype(bf)
    wuv = w_ukv3[:, :, MLA_NOPE_DIM:].reshape(MLA_KV_RANK, -1).astype(bf)
    pad_gain = lambda g: jnp.pad(g, (0, HEAD_PAD - MLA_QK_DIM)).reshape(1, HEAD_PAD)
    half = MLA_ROPE_DIM // 2
    inv_freq = 1.0 / (ROPE_THETA ** (jnp.arange(half, dtype=jnp.float32) * (2.0 / MLA_ROPE_DIM)))
    freq = jnp.concatenate([jnp.zeros((MLA_NOPE_DIM,), jnp.float32), inv_freq, inv_freq,
                            jnp.zeros((HEAD_PAD - MLA_QK_DIM,), jnp.float32)]).reshape(1, HEAD_PAD)

    sbq, sbk, sbv, q, k, v = _prep_call(
        x2d, pos, row(an), win, row(cqn), wuq, row(ckvn), wuk, wuv,
        pad_gain(qn), pad_gain(kn), freq)

    r3 = lambda a: a.reshape(b, s, a.shape[-1])
    sb = _sb_call(r3(sbq), r3(sbk), r3(sbv)).reshape(b * s, SB_WIDTH)
    vt = v.reshape(b, s // ATT_BLOCK, MLA_WIDTH, ATT_BLOCK)
    mla = _mla_call(r3(q), r3(k), vt).reshape(b * s, MLA_WIDTH)

    keys = sub_keys.transpose(1, 0, 2, 3).reshape(2 * PEER_HEADS, PEER_N_KEYS, PEER_HALF_DIM).astype(bf)
    x2, hn, scores = _route_call(
        x2d, sb, mla, row(sbn), row(mlan), w_o.astype(bf), row(fn), w_pq.astype(bf), keys)

    slab = lambda w: w.reshape(PEER_N_EXPERTS, D_MODEL // 128, 128)
    uv = jnp.concatenate([slab(pu), slab(pv)], axis=1).reshape(PEER_N_EXPERTS * SLAB_ROWS, 128)
    return _expert_call(uv, hn, x2, scores)


def kernel(x, positions, attn_norm, w_in, cq_norm, w_uq, ckv_norm, w_ukv, q_norm, k_norm,
           sb_out_norm, mla_out_norm, w_o, ffn_norm, peer_w_q, peer_sub_keys, peer_u, peer_v):
    b, s, d = x.shape
    x2d = x.reshape(b * s, d)
    pos = positions.reshape(b * s, 1)
    for l in range(attn_norm.shape[0]):
        x2d = _layer(x2d, pos, b, s, attn_norm[l], w_in[l], cq_norm[l], w_uq[l], ckv_norm[l],
                     w_ukv[l], q_norm[l], k_norm[l], sb_out_norm[l], mla_out_norm[l], w_o[l],
                     ffn_norm[l], peer_w_q[l], peer_sub_keys[l], peer_u[l], peer_v[l])
    return x2d.reshape(b, s, d)
```

```python
import math

import jax
import jax.numpy as jnp
from jax import lax
from jax.experimental import pallas as pl
from jax.experimental.pallas import tpu as pltpu

D_MODEL = 1024
CHUNK = 64
EPS = 1e-6

SB_HEADS = 8
SB_HEAD_DIM = 64
SB_WIDTH = SB_HEADS * SB_HEAD_DIM

MLA_HEADS = 8
MLA_NOPE_DIM = 64
MLA_ROPE_DIM = 32
MLA_QK_DIM = MLA_NOPE_DIM + MLA_ROPE_DIM
MLA_V_DIM = 64
MLA_Q_RANK = 384
MLA_KV_RANK = 256
MLA_WIDTH = MLA_HEADS * MLA_V_DIM
ROPE_THETA = 10000.0
HEAD_PAD = 128
MLA_PAD_WIDTH = MLA_HEADS * HEAD_PAD

PEER_HEADS = 8
PEER_N_KEYS = 128
PEER_N_EXPERTS = PEER_N_KEYS * PEER_N_KEYS
PEER_KEY_DIM = 256
PEER_HALF_DIM = PEER_KEY_DIM // 2
PEER_TOPK = 16
PEER_PICKS = PEER_HEADS * PEER_TOPK

_C_SBQ = 0
_C_SBK = SB_WIDTH
_C_SBV = 2 * SB_WIDTH
_C_CQ = 3 * SB_WIDTH
_C_CKV = _C_CQ + MLA_Q_RANK
_C_KPE = _C_CKV + MLA_KV_RANK
IN_PAD_WIDTH = _C_KPE + HEAD_PAD

VMEM_LIMIT = 56 * 1024 * 1024

PREP_TOKENS = 512
ATT_BLOCK = 256
ATT_HEAD_GROUP = 8
MLA_HEAD_GROUP = 8
EXPERT_TOKENS = 256
GATHER_UNROLL = 4
GATHER_AHEAD = 3
GATHER_SLOTS = (GATHER_AHEAD + 1) * GATHER_UNROLL
SLAB_ROWS = 2 * D_MODEL // 128
SLAB_PITCH = SLAB_ROWS + 1
SLOT_ROWS = PEER_PICKS * SLAB_PITCH

LOG2E = math.log2(math.e)

SB_DEAD_LOG = -105.0


def _rms(x, g):
    ms = jnp.mean(x * x, axis=-1, keepdims=True)
    return x * lax.rsqrt(ms + EPS) * g


def _prep_kernel(x_ref, pos_ref, an_ref, win_ref, cqn_ref, wuq_ref, ckvn_ref,
                 wuk_ref, wuv_ref, qn_ref, kn_ref, freq_ref,
                 sbq_ref, sbk_ref, sbv_ref, q_ref, k_ref, vt_ref):
    h = _rms(x_ref[...], an_ref[...]).astype(jnp.bfloat16)
    proj = jnp.dot(h, win_ref[...], preferred_element_type=jnp.float32)
    sbq_ref[...] = (proj[:, _C_SBQ:_C_SBQ + SB_WIDTH] * (SB_HEAD_DIM ** -0.5)).astype(jnp.bfloat16)
    sbk_ref[...] = proj[:, _C_SBK:_C_SBK + SB_WIDTH].astype(jnp.bfloat16)
    sbv_ref[...] = proj[:, _C_SBV:_C_SBV + SB_WIDTH].astype(jnp.bfloat16)

    cq = _rms(proj[:, _C_CQ:_C_CQ + MLA_Q_RANK], cqn_ref[...]).astype(jnp.bfloat16)
    ckv = _rms(proj[:, _C_CKV:_C_CKV + MLA_KV_RANK], ckvn_ref[...]).astype(jnp.bfloat16)
    kpe = proj[:, _C_KPE:_C_KPE + HEAD_PAD]
    qf = jnp.dot(cq, wuq_ref[...], preferred_element_type=jnp.float32)
    kf = jnp.dot(ckv, wuk_ref[...], preferred_element_type=jnp.float32)
    vf = jnp.dot(ckv, wuv_ref[...], preferred_element_type=jnp.float32)
    for blk in range(vt_ref.shape[0]):
        vt_ref[blk] = vf[blk * ATT_BLOCK:(blk + 1) * ATT_BLOCK, :].T.astype(jnp.bfloat16)

    ang = pos_ref[...].astype(jnp.float32) * freq_ref[...]
    cos = jnp.cos(ang)
    sin = jnp.sin(ang)
    lane = lax.broadcasted_iota(jnp.int32, ang.shape, 1)
    half = MLA_ROPE_DIM // 2
    lo = (lane >= MLA_NOPE_DIM) & (lane < MLA_NOPE_DIM + half)
    hi = (lane >= MLA_NOPE_DIM + half) & (lane < MLA_QK_DIM)
    sin_lo = jnp.where(lo, -sin, 0.0)
    sin_hi = jnp.where(hi, sin, 0.0)

    def norm_rope(xh, g):
        ms = jnp.sum(xh * xh, axis=-1, keepdims=True) * (1.0 / MLA_QK_DIM)
        xn = xh * lax.rsqrt(ms + EPS) * g
        return (xn * cos + pltpu.roll(xn, HEAD_PAD - half, 1) * sin_lo
                + pltpu.roll(xn, half, 1) * sin_hi)

    for hd in range(MLA_HEADS):
        sl = slice(hd * HEAD_PAD, (hd + 1) * HEAD_PAD)
        q_ref[:, sl] = norm_rope(qf[:, sl], qn_ref[...]).astype(jnp.bfloat16)
        k_ref[:, sl] = norm_rope(kf[:, sl] + kpe, kn_ref[...]).astype(jnp.bfloat16)


def _prep_call(x2d, pos, an, win, cqn, wuq, ckvn, wuk, wuv, qn, kn, freq):
    n = x2d.shape[0]
    t = PREP_TOKENS
    full = lambda a: pl.BlockSpec(a.shape, lambda i: (0,) * a.ndim)
    tok = lambda w: pl.BlockSpec((t, w), lambda i: (i, 0))
    bf = jnp.bfloat16
    return pl.pallas_call(
        _prep_kernel,
        grid=(n // t,),
        in_specs=[tok(D_MODEL), tok(1), full(an), full(win), full(cqn), full(wuq),
                  full(ckvn), full(wuk), full(wuv), full(qn), full(kn), full(freq)],
        out_specs=[tok(SB_WIDTH), tok(SB_WIDTH), tok(SB_WIDTH),
                   tok(MLA_PAD_WIDTH), tok(MLA_PAD_WIDTH),
                   pl.BlockSpec((t // ATT_BLOCK, MLA_WIDTH, ATT_BLOCK), lambda i: (i, 0, 0))],
        out_shape=[jax.ShapeDtypeStruct((n, SB_WIDTH), bf)] * 3
        + [jax.ShapeDtypeStruct((n, MLA_PAD_WIDTH), bf)] * 2
        + [jax.ShapeDtypeStruct((n // ATT_BLOCK, MLA_WIDTH, ATT_BLOCK), bf)],
        compiler_params=pltpu.CompilerParams(
            dimension_semantics=("parallel",), vmem_limit_bytes=VMEM_LIMIT),
        name="prep",
    )(x2d, pos, an, win, cqn, wuq, ckvn, wuk, wuv, qn, kn, freq)


def _sb_kernel(q_ref, k_ref, v_ref, o_ref):
    i = pl.program_id(1)
    tb = ATT_BLOCK
    row = lax.broadcasted_iota(jnp.int32, (tb, tb), 0)
    col = lax.broadcasted_iota(jnp.int32, (tb, tb), 1)
    strict = col < row
    upper = jnp.where(row > col, 1.0, 0.0).astype(jnp.bfloat16)

    for grp in range(SB_HEADS // ATT_HEAD_GROUP):
        heads = range(grp * ATT_HEAD_GROUP, (grp + 1) * ATT_HEAD_GROUP)

        def walk(kb, st, masked):
            start = pl.multiple_of(kb * tb, tb)
            heads_sl = [slice(hd * SB_HEAD_DIM, (hd + 1) * SB_HEAD_DIM) for hd in heads]
            zs = [lax.dot_general(q_ref[0, :, sl], k_ref[0, pl.ds(start, tb), sl],
                                  (((1,), (1,)), ((), ())), preferred_element_type=jnp.float32)
                  for sl in heads_sl]
            mids = []
            for z in zs:
                lf_all = -(jnp.maximum(z, 0.0) + jnp.log(1.0 + jnp.exp(-jnp.abs(z))))
                lf = jnp.where(strict, lf_all, 0.0) if masked else lf_all
                lf_hi = lf.astype(jnp.bfloat16)
                lf_lo = (lf - lf_hi.astype(jnp.float32)).astype(jnp.bfloat16)
                mids.append((z + lf_all, lf_hi, lf_lo, jnp.sum(lf, axis=-1, keepdims=True)))
            laters = [jnp.dot(hi, upper, preferred_element_type=jnp.float32)
                      + jnp.dot(lo, upper, preferred_element_type=jnp.float32)
                      for _, hi, lo, _ in mids]
            weights = []
            for (ls, _, _, _), later, (carry, _) in zip(mids, laters, st):
                a = jnp.exp(ls + later + carry)
                if masked:
                    a = jnp.where(strict, a, 0.0)
                weights.append(a.astype(jnp.bfloat16))
            return [(carry + rs,
                     acc + jnp.dot(a, v_ref[0, pl.ds(start, tb), sl],
                                   preferred_element_type=jnp.float32))
                    for a, sl, (_, _, _, rs), (carry, acc) in zip(weights, heads_sl, mids, st)]

        carry0 = jnp.zeros((tb, 1), jnp.float32)
        acc0 = jnp.zeros((tb, SB_HEAD_DIM), jnp.float32)
        state = walk(i, [(carry0, acc0)] * len(heads), True)

        def alive(st):
            top = st[0][0]
            for c, _ in st[1:]:
                top = jnp.maximum(top, c)
            return jnp.max(top) > SB_DEAD_LOG

        def cond(s):
            return (s[0] < i) & s[1]

        def body(s):
            j, _, st = s
            st = walk(i - 1 - j, st, False)
            return j + 1, alive(st), st

        _, _, state = lax.while_loop(cond, body, (0, alive(state), state))
        for hd, (_, acc) in zip(heads, state):
            o_ref[0, :, hd * SB_HEAD_DIM:(hd + 1) * SB_HEAD_DIM] = acc


def _sb_call(q, k, v):
    b, s, w = q.shape
    tb = ATT_BLOCK
    return pl.pallas_call(
        _sb_kernel,
        grid=(b, s // tb),
        in_specs=[pl.BlockSpec((1, tb, w), lambda bi, i: (bi, i, 0)),
                  pl.BlockSpec((1, s, w), lambda bi, i: (bi, 0, 0)),
                  pl.BlockSpec((1, s, w), lambda bi, i: (bi, 0, 0))],
        out_specs=pl.BlockSpec((1, tb, w), lambda bi, i: (bi, i, 0)),
        out_shape=jax.ShapeDtypeStruct((b, s, w), jnp.float32),
        compiler_params=pltpu.CompilerParams(
            dimension_semantics=("parallel", "arbitrary"), vmem_limit_bytes=VMEM_LIMIT),
        name="sb_attn",
    )(q, k, v)


def _mla_kernel(q_ref, k_ref, vt_ref, o_ref):
    i = pl.program_id(1)
    tb = ATT_BLOCK
    key = lax.broadcasted_iota(jnp.int32, (tb, tb), 0)
    qry = lax.broadcasted_iota(jnp.int32, (tb, tb), 1)
    allowed = (key // CHUNK) <= (qry // CHUNK)
    scale = MLA_QK_DIM ** -0.5

    def scores(hd, kb):
        qh = q_ref[0, :, hd * HEAD_PAD:(hd + 1) * HEAD_PAD]
        start = pl.multiple_of(kb * tb, tb)
        ks = k_ref[0, pl.ds(start, tb), hd * HEAD_PAD:(hd + 1) * HEAD_PAD]
        vt = vt_ref[0, kb, hd * MLA_V_DIM:(hd + 1) * MLA_V_DIM, :]
        st = lax.dot_general(ks, qh, (((1,), (1,)), ((), ())),
                             preferred_element_type=jnp.float32) * (scale * LOG2E)
        return st, vt

    outs = []
    for grp in range(MLA_HEADS // MLA_HEAD_GROUP):
        heads = range(grp * MLA_HEAD_GROUP, (grp + 1) * MLA_HEAD_GROUP)

        def walk(kb, st, masked):
            tiles = [scores(hd, kb) for hd in heads]
            mids = []
            for (s_t, vt), (m, l, acc) in zip(tiles, st):
                if masked:
                    s_t = jnp.where(allowed, s_t, -jnp.inf)
                m_new = jnp.maximum(m, jnp.max(s_t, axis=0, keepdims=True))
                alpha = jnp.exp2(m - m_new)
                p = jnp.exp2(s_t - m_new)
                l = alpha * l + jnp.sum(p, axis=0, keepdims=True)
                mids.append((m_new, l, alpha, p.astype(jnp.bfloat16), vt, acc))
            return [(m, l, alpha * acc + jnp.dot(vt, p, preferred_element_type=jnp.float32))
                    for m, l, alpha, p, vt, acc in mids]

        empty = (jnp.full((1, tb), -jnp.inf, jnp.float32), jnp.zeros((1, tb), jnp.float32),
                 jnp.zeros((MLA_V_DIM, tb), jnp.float32))
        state = walk(i, [empty] * len(heads), True)
        state = lax.fori_loop(0, i, lambda kb, st: walk(kb, st, False), state)
        outs += [acc / l for _, l, acc in state]
    o_ref[0] = jnp.concatenate(outs, axis=0).T


def _mla_call(q, k, vt):
    b, s, wq = q.shape
    _, nblk, wv, tb = vt.shape
    return pl.pallas_call(
        _mla_kernel,
        grid=(b, s // tb),
        in_specs=[pl.BlockSpec((1, tb, wq), lambda bi, i: (bi, i, 0)),
                  pl.BlockSpec((1, s, wq), lambda bi, i: (bi, 0, 0)),
                  pl.BlockSpec((1, nblk, wv, tb), lambda bi, i: (bi, 0, 0, 0))],
        out_specs=pl.BlockSpec((1, tb, wv), lambda bi, i: (bi, i, 0)),
        out_shape=jax.ShapeDtypeStruct((b, s, wv), jnp.float32),
        compiler_params=pltpu.CompilerParams(
            dimension_semantics=("parallel", "arbitrary"), vmem_limit_bytes=VMEM_LIMIT),
        name="mla_attn",
    )(q, k, vt)


def _topk_rows(a, k, ids=None):
    r, t = a.shape
    rows = lax.broadcasted_iota(jnp.int32, (r, t), 0) if ids is None else ids
    krow = lax.broadcasted_iota(jnp.int32, (k, t), 0)
    big = jnp.iinfo(jnp.int32).max

    def body(it, c):
        a, vals, idxs = c
        m = jnp.max(a, axis=0, keepdims=True)
        am = jnp.min(jnp.where(a == m, rows, big), axis=0, keepdims=True)
        a = jnp.where(rows == am, -jnp.inf, a)
        vals = jnp.where(krow == it, m, vals)
        idxs = jnp.where(krow == it, am, idxs)
        return a, vals, idxs

    _, vals, idxs = lax.fori_loop(
        0, k, body, (a, jnp.zeros((k, t), jnp.float32), jnp.zeros((k, t), jnp.int32)))
    return vals, idxs


def _select_rows(sel, table):
    out = jnp.zeros(sel.shape, table.dtype)
    for r in range(table.shape[0]):
        out = jnp.where(sel == r, table[r:r + 1, :], out)
    return out


def _candidate_tiles(k):
    tiles = [("row", 0, s) for s in range(0, k, 8)] + [("col", 0, s) for s in range(0, k, 8)]
    f = 1
    while (f + 1) * (f + 1) <= k:
        tiles += [("row", f, 0), ("col", f, 0)]
        f += 1
    seen, used, ids = set(), [], []
    for kind, fixed, start in tiles:
        tile_ids = []
        for r in range(8):
            pair = (fixed, start + r) if kind == "row" else (start + r, fixed)
            ok = (pair[0] + 1) * (pair[1] + 1) <= k and pair not in seen
            seen.add(pair)
            tile_ids.append(pair[0] * k + pair[1] if ok else -1)
        if max(tile_ids) >= 0:
            used.append((kind, fixed, start))
            ids += tile_ids
    want = sum(1 for a in range(k) for b in range(k) if (a + 1) * (b + 1) <= k)
    assert sum(i >= 0 for i in ids) == want, "candidate tiles must cover the whole staircase"
    return used, ids


_CAND_TILES, _CAND_IDS = _candidate_tiles(PEER_TOPK)


def _route_kernel(x_ref, sb_ref, mla_ref, sbn_ref, mlan_ref, wo_ref, fn_ref, wq_ref,
                  keys_ref, x2_ref, hn_ref, score_ref):
    bf = jnp.bfloat16
    sbn = _rms(sb_ref[...], sbn_ref[...]).astype(bf)
    mlan = _rms(mla_ref[...], mlan_ref[...]).astype(bf)
    attn = (jnp.dot(sbn, wo_ref[0:SB_WIDTH, :], preferred_element_type=jnp.float32)
            + jnp.dot(mlan, wo_ref[SB_WIDTH:SB_WIDTH + MLA_WIDTH, :],
                      preferred_element_type=jnp.float32))
    x2 = x_ref[...] + attn
    x2_ref[...] = x2
    hn = _rms(x2, fn_ref[...])
    hn_ref[...] = hn
    qp = jnp.dot(hn.astype(bf), wq_ref[...], preferred_element_type=jnp.float32)
    for hc in range(2 * PEER_HEADS):
        qhc = qp[:, hc * PEER_HALF_DIM:(hc + 1) * PEER_HALF_DIM].astype(bf)
        score_ref[0, hc] = lax.dot_general(keys_ref[hc], qhc, (((1,), (1,)), ((), ())),
                                           preferred_element_type=jnp.float32)


def _route_call(x2d, sb, mla, sbn, mlan, wo, fn, wq, keys):
    n = x2d.shape[0]
    t = EXPERT_TOKENS
    full = lambda a: pl.BlockSpec(a.shape, lambda i: (0,) * a.ndim)
    tok = lambda w: pl.BlockSpec((t, w), lambda i: (i, 0))
    return pl.pallas_call(
        _route_kernel,
        grid=(n // t,),
        in_specs=[tok(D_MODEL), tok(SB_WIDTH), tok(MLA_WIDTH), full(sbn), full(mlan),
                  full(wo), full(fn), full(wq), full(keys)],
        out_specs=[tok(D_MODEL), tok(D_MODEL),
                   pl.BlockSpec((1, 2 * PEER_HEADS, PEER_N_KEYS, t), lambda i: (i, 0, 0, 0))],
        out_shape=[jax.ShapeDtypeStruct((n, D_MODEL), jnp.float32),
                   jax.ShapeDtypeStruct((n, D_MODEL), jnp.float32),
                   jax.ShapeDtypeStruct((n // t, 2 * PEER_HEADS, PEER_N_KEYS, t), jnp.float32)],
        compiler_params=pltpu.CompilerParams(
            dimension_semantics=("parallel",), vmem_limit_bytes=VMEM_LIMIT),
        name="route",
    )(x2d, sb, mla, sbn, mlan, wo, fn, wq, keys)


def _expert_kernel(uv_hbm, hn_ref, x2_ref, sc0_ref, scn_ref, ids_ref, o_ref,
                   idx_vm, idx_smem, gate_vm, buf_even, buf_odd, idx_sem, row_sem):
    step = pl.program_id(0)
    nsteps = pl.num_programs(0)
    tb = EXPERT_TOKENS
    nu = GATHER_UNROLL
    ahead_b = GATHER_AHEAD
    per_buf = GATHER_SLOTS // 2
    bufs = (buf_even, buf_odd)
    head_bunches = tb // nu // PEER_HEADS
    assert ahead_b % 2 == 1 and GATHER_SLOTS == (ahead_b + 1) * nu
    assert (tb // nu) % (2 * (per_buf // nu)) == 0
    assert head_bunches % 2 == 0 and head_bunches >= 7
    cur = step % 2
    has_next = step + 1 < nsteps
    k = PEER_TOPK

    def candidates(s0, s1):
        pieces = []
        for kind, fixed, start in _CAND_TILES:
            if kind == "row":
                pieces.append(s0[fixed:fixed + 1, :] + s1[start:start + 8, :])
            else:
                pieces.append(s0[start:start + 8, :] + s1[fixed:fixed + 1, :])
        return jnp.where(ids_ref[...] >= 0, jnp.concatenate(pieces, axis=0), -jnp.inf)

    def finish(hd, slot, best, pos, i0, i1):
        e0 = _select_rows(pos // k, i0)
        e1 = _select_rows(pos % k, i1)
        p = jnp.exp(best - best[0:1, :])
        gate = p / jnp.sum(p, axis=0, keepdims=True)
        rows = pl.ds(hd * k if isinstance(hd, int) else pl.multiple_of(hd * k, k), k)
        idx_vm[rows, :] = e0 * PEER_N_KEYS + e1
        for tile in range(tb // 128):
            gate_vm[slot, tile, rows, :] = gate[:, tile * 128:(tile + 1) * 128]

    def topk_steps(state, its, ids=None):
        a, vals, idxs = state
        rows = lax.broadcasted_iota(jnp.int32, a.shape, 0) if ids is None else ids
        krow = lax.broadcasted_iota(jnp.int32, vals.shape, 0)
        big = jnp.iinfo(jnp.int32).max
        for it in its:
            m = jnp.max(a, axis=0, keepdims=True)
            am = jnp.min(jnp.where(a == m, rows, big), axis=0, keepdims=True)
            a = jnp.where(rows == am, -jnp.inf, a)
            vals = jnp.where(krow == it, m, vals)
            idxs = jnp.where(krow == it, am, idxs)
        return a, vals, idxs

    def fresh(a):
        return a, jnp.zeros((k, tb), jnp.float32), jnp.zeros((k, tb), jnp.int32)

    def idx_copy():
        return pltpu.make_async_copy(idx_vm, idx_smem, idx_sem)

    def place(b, j):
        par, half = b
        return par, (half % (per_buf // nu)) * nu + j

    def start_token(t, par, slot):
        base = slot * SLOT_ROWS
        for p in range(PEER_PICKS):
            src = pl.multiple_of(idx_smem[p, t] * SLAB_ROWS, SLAB_ROWS)
            pltpu.make_async_copy(
                uv_hbm.at[pl.ds(src, SLAB_ROWS), :],
                bufs[par].at[pl.ds(base + p * SLAB_PITCH, SLAB_ROWS), :],
                row_sem.at[par * per_buf + slot]).start(priority=p % 2)

    def wait_token(par, slot):
        rows = PEER_PICKS * SLAB_ROWS
        pltpu.make_async_copy(uv_hbm.at[pl.ds(0, rows), :],
                              bufs[par].at[pl.ds(slot * SLOT_ROWS, rows), :],
                              row_sem.at[par * per_buf + slot]).wait()

    @pl.when(step == 0)
    def _():
        def head(hd, _):
            s0, i0 = _topk_rows(sc0_ref[0, 2 * hd], k)
            s1, i1 = _topk_rows(sc0_ref[0, 2 * hd + 1], k)
            best, pos = _topk_rows(candidates(s0, s1), k, ids_ref[...])
            finish(hd, 0, best, pos, i0, i1)
            return 0

        lax.fori_loop(0, PEER_HEADS, head, 0)
        idx_copy().start()

    idx_copy().wait()
    for b in range(ahead_b):
        for j in range(nu):
            start_token(b * nu + j, *place((b % 2, b // 2), j))

    lane = lax.broadcasted_iota(jnp.int32, (PEER_PICKS, 128), 1)
    groups = PEER_PICKS // 8
    tiles = D_MODEL // 128

    def compute_token(t, par, slot):
        buf = bufs[par]
        base = slot * SLOT_ROWS
        xrow = hn_ref[pl.ds(t, 1), :]
        pres = []
        for g in range(groups):
            acc = None
            for c in range(tiles):
                u = buf[pl.ds(base + g * 8 * SLAB_PITCH + c, 8, stride=SLAB_PITCH), :]
                term = u * xrow[:, c * 128:(c + 1) * 128]
                acc = term if acc is None else acc + term
            pres.append(jnp.sum(acc, axis=1, keepdims=True))
        pre = jnp.concatenate(pres, axis=0)
        gate = jnp.sum(jnp.where(lane == t % 128, gate_vm[cur, t // 128], 0.0),
                       axis=1, keepdims=True)
        coef = gate * (0.5 * pre * (1.0 + lax.erf(pre * (2.0 ** -0.5))))
        outs = []
        for c in range(tiles):
            acc = None
            for g in range(groups):
                v = buf[pl.ds(base + g * 8 * SLAB_PITCH + tiles + c, 8, stride=SLAB_PITCH), :]
                term = coef[g * 8:(g + 1) * 8, :] * v
                acc = term if acc is None else acc + term
            outs.append(jnp.sum(acc, axis=0, keepdims=True))
        out = jnp.concatenate(outs, axis=1)
        o_ref[pl.ds(t, 1), :] = x2_ref[pl.ds(t, 1), :] + out

    def bunch(par, half, refill, search=None):
        first = (2 * half + par) * nu
        for j in range(nu):
            wait_token(*place((par, half), j))
        found = search() if search is not None else None
        for j in range(nu):
            compute_token(first + j, *place((par, half), j))

        if refill:
            later = (par + ahead_b) // 2 + half
            for j in range(nu):
                start_token(first + ahead_b * nu + j, *place((1 - par, later), j))
        return found

    half_its = (range(0, k // 2), range(k // 2, k))

    def head_body(hd, last):
        def run(j, search=None):
            return bunch(j % 2, hd * (head_bunches // 2) + j // 2,
                         not last or j < head_bunches - ahead_b, search)

        st0 = fresh(scn_ref[0, 2 * hd])
        st1 = fresh(scn_ref[0, 2 * hd + 1])
        st0 = run(0, lambda: topk_steps(st0, half_its[0]))
        st0 = run(1, lambda: topk_steps(st0, half_its[1]))
        st1 = run(2, lambda: topk_steps(st1, half_its[0]))
        st1 = run(3, lambda: topk_steps(st1, half_its[1]))
        ids = ids_ref[...]
        st2 = fresh(candidates(st0[1], st1[1]))
        st2 = run(4, lambda: topk_steps(st2, half_its[0], ids))
        st2 = run(5, lambda: topk_steps(st2, half_its[1], ids))
        run(6, lambda: finish(hd, 1 - cur, st2[1], st2[2], st0[2], st1[2]))
        for j in range(7, head_bunches):
            run(j)
        return 0

    lax.fori_loop(0, PEER_HEADS - 1, lambda hd, c: head_body(hd, False), 0)
    head_body(PEER_HEADS - 1, True)

    @pl.when(has_next)
    def _():
        idx_copy().start()


def _expert_call(uv, hn, x2, scores):
    n = hn.shape[0]
    tb = EXPERT_TOKENS
    nblk = n // tb
    tok = pl.BlockSpec((tb, D_MODEL), lambda i: (i, 0))
    sc_shape = (1,) + scores.shape[1:]
    ids = jnp.broadcast_to(jnp.asarray(_CAND_IDS, jnp.int32)[:, None], (len(_CAND_IDS), tb))
    return pl.pallas_call(
        _expert_kernel,
        grid=(nblk,),
        in_specs=[pl.BlockSpec(memory_space=pl.ANY),
                  tok, tok,
                  pl.BlockSpec(sc_shape, lambda i: (0, 0, 0, 0)),
                  pl.BlockSpec(sc_shape, lambda i: (jnp.minimum(i + 1, nblk - 1), 0, 0, 0)),
                  pl.BlockSpec(ids.shape, lambda i: (0, 0))],
        out_specs=tok,
        out_shape=jax.ShapeDtypeStruct((n, D_MODEL), jnp.float32),
        scratch_shapes=[pltpu.VMEM((PEER_PICKS, tb), jnp.int32),
                        pltpu.SMEM((PEER_PICKS, tb), jnp.int32),
                        pltpu.VMEM((2, tb // 128, PEER_PICKS, 128), jnp.float32),
                        pltpu.VMEM((GATHER_SLOTS // 2 * SLOT_ROWS, 128), jnp.float32),
                        pltpu.VMEM((GATHER_SLOTS // 2 * SLOT_ROWS, 128), jnp.float32),
                        pltpu.SemaphoreType.DMA,
                        pltpu.SemaphoreType.DMA((GATHER_SLOTS,))],
        compiler_params=pltpu.CompilerParams(
            dimension_semantics=("arbitrary",), vmem_limit_bytes=VMEM_LIMIT),
        name="experts",
    )(uv, hn, x2, scores, scores, ids)


def _pad_heads(w, head_dim):
    k = w.shape[0]
    w = w.reshape(k, -1, head_dim)
    w = jnp.pad(w, ((0, 0), (0, 0), (0, HEAD_PAD - head_dim)))
    return w.reshape(k, -1)


def _layer(x2d, pos, b, s, an, w_in, cqn, w_uq, ckvn, w_ukv, qn, kn, sbn, mlan, w_o, fn,
           w_pq, sub_keys, pu, pv):
    bf = jnp.bfloat16
    row = lambda a: a.reshape(1, -1)
    kpe_cols = jnp.pad(w_in[:, _C_KPE:_C_KPE + MLA_ROPE_DIM],
                       ((0, 0), (MLA_NOPE_DIM, HEAD_PAD - MLA_QK_DIM)))
    win = jnp.concatenate([w_in[:, :_C_KPE], kpe_cols], axis=1).astype(bf)
    assert win.shape == (D_MODEL, IN_PAD_WIDTH)
    wuq = _pad_heads(w_uq, MLA_QK_DIM).astype(bf)
    w_ukv3 = w_ukv.reshape(MLA_KV_RANK, MLA_HEADS, MLA_NOPE_DIM + MLA_V_DIM)
    wuk = _pad_heads(w_ukv3[:, :, :MLA_NOPE_DIM].reshape(MLA_KV_RANK, -1), MLA_NOPE_DIM).astype(bf)
    wuv = w_ukv3[:, :, MLA_NOPE_DIM:].reshape(MLA_KV_RANK, -1).astype(bf)
    pad_gain = lambda g: jnp.pad(g, (0, HEAD_PAD - MLA_QK_DIM)).reshape(1, HEAD_PAD)
    half = MLA_ROPE_DIM // 2
    inv_freq = 1.0 / (ROPE_THETA ** (jnp.arange(half, dtype=jnp.float32) * (2.0 / MLA_ROPE_DIM)))
    freq = jnp.concatenate([jnp.zeros((MLA_NOPE_DIM,), jnp.float32), inv_freq, inv_freq,
                            jnp.zeros((HEAD_PAD - MLA_QK_DIM,), jnp.float32)]).reshape(1, HEAD_PAD)

    sbq, sbk, sbv, q, k, v = _prep_call(
        x2d, pos, row(an), win, row(cqn), wuq, row(ckvn), wuk, wuv,
        pad_gain(qn), pad_gain(kn), freq)

    r3 = lambda a: a.reshape(b, s, a.shape[-1])
    sb = _sb_call(r3(sbq), r3(sbk), r3(sbv)).reshape(b * s, SB_WIDTH)
    vt = v.reshape(b, s // ATT_BLOCK, MLA_WIDTH, ATT_BLOCK)
    mla = _mla_call(r3(q), r3(k), vt).reshape(b * s, MLA_WIDTH)

    keys = sub_keys.transpose(1, 0, 2, 3).reshape(2 * PEER_HEADS, PEER_N_KEYS, PEER_HALF_DIM).astype(bf)
    x2, hn, scores = _route_call(
        x2d, sb, mla, row(sbn), row(mlan), w_o.astype(bf), row(fn), w_pq.astype(bf), keys)

    slab = lambda w: w.reshape(PEER_N_EXPERTS, D_MODEL // 128, 128)
    uv = jnp.concatenate([slab(pu), slab(pv)], axis=1).reshape(PEER_N_EXPERTS * SLAB_ROWS, 128)
    return _expert_call(uv, hn, x2, scores)


def kernel(x, positions, attn_norm, w_in, cq_norm, w_uq, ckv_norm, w_ukv, q_norm, k_norm,
           sb_out_norm, mla_out_norm, w_o, ffn_norm, peer_w_q, peer_sub_keys, peer_u, peer_v):
    b, s, d = x.shape
    x2d = x.reshape(b * s, d)
    pos = positions.reshape(b * s, 1)
    for l in range(attn_norm.shape[0]):
        x2d = _layer(x2d, pos, b, s, attn_norm[l], w_in[l], cq_norm[l], w_uq[l], ckv_norm[l],
                     w_ukv[l], q_norm[l], k_norm[l], sb_out_norm[l], mla_out_norm[l], w_o[l],
                     ffn_norm[l], peer_w_q[l], peer_sub_keys[l], peer_u[l], peer_v[l])
    return x2d.reshape(b, s, d)
```

```python
import math

import jax
import jax.numpy as jnp
from jax import lax
from jax.experimental import pallas as pl
from jax.experimental.pallas import tpu as pltpu

D_MODEL = 1024
CHUNK = 64
EPS = 1e-6

SB_HEADS = 8
SB_HEAD_DIM = 64
SB_WIDTH = SB_HEADS * SB_HEAD_DIM

MLA_HEADS = 8
MLA_NOPE_DIM = 64
MLA_ROPE_DIM = 32
MLA_QK_DIM = MLA_NOPE_DIM + MLA_ROPE_DIM
MLA_V_DIM = 64
MLA_Q_RANK = 384
MLA_KV_RANK = 256
MLA_WIDTH = MLA_HEADS * MLA_V_DIM
ROPE_THETA = 10000.0
HEAD_PAD = 128
MLA_PAD_WIDTH = MLA_HEADS * HEAD_PAD

PEER_HEADS = 8
PEER_N_KEYS = 128
PEER_N_EXPERTS = PEER_N_KEYS * PEER_N_KEYS
PEER_KEY_DIM = 256
PEER_HALF_DIM = PEER_KEY_DIM // 2
PEER_TOPK = 16
PEER_PICKS = PEER_HEADS * PEER_TOPK

_C_SBQ = 0
_C_SBK = SB_WIDTH
_C_SBV = 2 * SB_WIDTH
_C_CQ = 3 * SB_WIDTH
_C_CKV = _C_CQ + MLA_Q_RANK
_C_KPE = _C_CKV + MLA_KV_RANK
IN_PAD_WIDTH = _C_KPE + HEAD_PAD

VMEM_LIMIT = 56 * 1024 * 1024

PREP_TOKENS = 512
ATT_BLOCK = 256
ATT_HEAD_GROUP = 8
MLA_HEAD_GROUP = 8
EXPERT_TOKENS = 256
GATHER_UNROLL = 4
GATHER_AHEAD = 3
GATHER_SLOTS = (GATHER_AHEAD + 1) * GATHER_UNROLL
SLAB_ROWS = 2 * D_MODEL // 128
SLAB_PITCH = SLAB_ROWS + 1
SLOT_ROWS = PEER_PICKS * SLAB_PITCH

LOG2E = math.log2(math.e)

SB_DEAD_LOG = -105.0


def _rms(x, g):
    ms = jnp.mean(x * x, axis=-1, keepdims=True)
    return x * lax.rsqrt(ms + EPS) * g


def _prep_kernel(x_ref, pos_ref, an_ref, win_ref, cqn_ref, wuq_ref, ckvn_ref,
                 wuk_ref, wuv_ref, qn_ref, kn_ref, freq_ref,
                 sbq_ref, sbk_ref, sbv_ref, q_ref, k_ref, vt_ref):
    h = _rms(x_ref[...], an_ref[...]).astype(jnp.bfloat16)
    proj = jnp.dot(h, win_ref[...], preferred_element_type=jnp.float32)
    sbq_ref[...] = (proj[:, _C_SBQ:_C_SBQ + SB_WIDTH] * (SB_HEAD_DIM ** -0.5)).astype(jnp.bfloat16)
    sbk_ref[...] = proj[:, _C_SBK:_C_SBK + SB_WIDTH].astype(jnp.bfloat16)
    sbv_ref[...] = proj[:, _C_SBV:_C_SBV + SB_WIDTH].astype(jnp.bfloat16)

    cq = _rms(proj[:, _C_CQ:_C_CQ + MLA_Q_RANK], cqn_ref[...]).astype(jnp.bfloat16)
    ckv = _rms(proj[:, _C_CKV:_C_CKV + MLA_KV_RANK], ckvn_ref[...]).astype(jnp.bfloat16)
    kpe = proj[:, _C_KPE:_C_KPE + HEAD_PAD]
    qf = jnp.dot(cq, wuq_ref[...], preferred_element_type=jnp.float32)
    kf = jnp.dot(ckv, wuk_ref[...], preferred_element_type=jnp.float32)
    vf = jnp.dot(ckv, wuv_ref[...], preferred_element_type=jnp.float32)
    for blk in range(vt_ref.shape[0]):
        vt_ref[blk] = vf[blk * ATT_BLOCK:(blk + 1) * ATT_BLOCK, :].T.astype(jnp.bfloat16)

    ang = pos_ref[...].astype(jnp.float32) * freq_ref[...]
    cos = jnp.cos(ang)
    sin = jnp.sin(ang)
    lane = lax.broadcasted_iota(jnp.int32, ang.shape, 1)
    half = MLA_ROPE_DIM // 2
    lo = (lane >= MLA_NOPE_DIM) & (lane < MLA_NOPE_DIM + half)
    hi = (lane >= MLA_NOPE_DIM + half) & (lane < MLA_QK_DIM)
    sin_lo = jnp.where(lo, -sin, 0.0)
    sin_hi = jnp.where(hi, sin, 0.0)

    def norm_rope(xh, g):
        ms = jnp.sum(xh * xh, axis=-1, keepdims=True) * (1.0 / MLA_QK_DIM)
        xn = xh * lax.rsqrt(ms + EPS) * g
        return (xn * cos + pltpu.roll(xn, HEAD_PAD - half, 1) * sin_lo
                + pltpu.roll(xn, half, 1) * sin_hi)

    for hd in range(MLA_HEADS):
        sl = slice(hd * HEAD_PAD, (hd + 1) * HEAD_PAD)
        q_ref[:, sl] = norm_rope(qf[:, sl], qn_ref[...]).astype(jnp.bfloat16)
        k_ref[:, sl] = norm_rope(kf[:, sl] + kpe, kn_ref[...]).astype(jnp.bfloat16)


def _prep_call(x2d, pos, an, win, cqn, wuq, ckvn, wuk, wuv, qn, kn, freq):
    n = x2d.shape[0]
    t = PREP_TOKENS
    full = lambda a: pl.BlockSpec(a.shape, lambda i: (0,) * a.ndim)
    tok = lambda w: pl.BlockSpec((t, w), lambda i: (i, 0))
    bf = jnp.bfloat16
    return pl.pallas_call(
        _prep_kernel,
        grid=(n // t,),
        in_specs=[tok(D_MODEL), tok(1), full(an), full(win), full(cqn), full(wuq),
                  full(ckvn), full(wuk), full(wuv), full(qn), full(kn), full(freq)],
        out_specs=[tok(SB_WIDTH), tok(SB_WIDTH), tok(SB_WIDTH),
                   tok(MLA_PAD_WIDTH), tok(MLA_PAD_WIDTH),
                   pl.BlockSpec((t // ATT_BLOCK, MLA_WIDTH, ATT_BLOCK), lambda i: (i, 0, 0))],
        out_shape=[jax.ShapeDtypeStruct((n, SB_WIDTH), bf)] * 3
        + [jax.ShapeDtypeStruct((n, MLA_PAD_WIDTH), bf)] * 2
        + [jax.ShapeDtypeStruct((n // ATT_BLOCK, MLA_WIDTH, ATT_BLOCK), bf)],
        compiler_params=pltpu.CompilerParams(
            dimension_semantics=("parallel",), vmem_limit_bytes=VMEM_LIMIT),
        name="prep",
    )(x2d, pos, an, win, cqn, wuq, ckvn, wuk, wuv, qn, kn, freq)


def _sb_kernel(q_ref, k_ref, v_ref, o_ref):
    i = pl.program_id(1)
    tb = ATT_BLOCK
    row = lax.broadcasted_iota(jnp.int32, (tb, tb), 0)
    col = lax.broadcasted_iota(jnp.int32, (tb, tb), 1)
    strict = col < row
    upper = jnp.where(row > col, 1.0, 0.0).astype(jnp.bfloat16)

    for grp in range(SB_HEADS // ATT_HEAD_GROUP):
        heads = range(grp * ATT_HEAD_GROUP, (grp + 1) * ATT_HEAD_GROUP)

        def walk(kb, st, masked):
            start = pl.multiple_of(kb * tb, tb)
            heads_sl = [slice(hd * SB_HEAD_DIM, (hd + 1) * SB_HEAD_DIM) for hd in heads]
            zs = [lax.dot_general(q_ref[0, :, sl], k_ref[0, pl.ds(start, tb), sl],
                                  (((1,), (1,)), ((), ())), preferred_element_type=jnp.float32)
                  for sl in heads_sl]
            mids = []
            for z in zs:
                lf_all = -(jnp.maximum(z, 0.0) + jnp.log(1.0 + jnp.exp(-jnp.abs(z))))
                lf = jnp.where(strict, lf_all, 0.0) if masked else lf_all
                lf_hi = lf.astype(jnp.bfloat16)
                lf_lo = (lf - lf_hi.astype(jnp.float32)).astype(jnp.bfloat16)
                mids.append((z + lf_all, lf_hi, lf_lo, jnp.sum(lf, axis=-1, keepdims=True)))
            laters = [jnp.dot(hi, upper, preferred_element_type=jnp.float32)
                      + jnp.dot(lo, upper, preferred_element_type=jnp.float32)
                      for _, hi, lo, _ in mids]
            weights = []
            for (ls, _, _, _), later, (carry, _) in zip(mids, laters, st):
                a = jnp.exp(ls + later + carry)
                if masked:
                    a = jnp.where(strict, a, 0.0)
                weights.append(a.astype(jnp.bfloat16))
            return [(carry + rs,
                     acc + jnp.dot(a, v_ref[0, pl.ds(start, tb), sl],
                                   preferred_element_type=jnp.float32))
                    for a, sl, (_, _, _, rs), (carry, acc) in zip(weights, heads_sl, mids, st)]

        carry0 = jnp.zeros((tb, 1), jnp.float32)
        acc0 = jnp.zeros((tb, SB_HEAD_DIM), jnp.float32)
        state = walk(i, [(carry0, acc0)] * len(heads), True)

        def alive(st):
            top = st[0][0]
            for c, _ in st[1:]:
                top = jnp.maximum(top, c)
            return jnp.max(top) > SB_DEAD_LOG

        def cond(s):
            return (s[0] < i) & s[1]

        def body(s):
            j, _, st = s
            st = walk(i - 1 - j, st, False)
            return j + 1, alive(st), st

        _, _, state = lax.while_loop(cond, body, (0, alive(state), state))
        for hd, (_, acc) in zip(heads, state):
            o_ref[0, :, hd * SB_HEAD_DIM:(hd + 1) * SB_HEAD_DIM] = acc


def _sb_call(q, k, v):
    b, s, w = q.shape
    tb = ATT_BLOCK
    return pl.pallas_call(
        _sb_kernel,
        grid=(b, s // tb),
        in_specs=[pl.BlockSpec((1, tb, w), lambda bi, i: (bi, i, 0)),
                  pl.BlockSpec((1, s, w), lambda bi, i: (bi, 0, 0)),
                  pl.BlockSpec((1, s, w), lambda bi, i: (bi, 0, 0))],
        out_specs=pl.BlockSpec((1, tb, w), lambda bi, i: (bi, i, 0)),
        out_shape=jax.ShapeDtypeStruct((b, s, w), jnp.float32),
        compiler_params=pltpu.CompilerParams(
            dimension_semantics=("parallel", "arbitrary"), vmem_limit_bytes=VMEM_LIMIT),
        name="sb_attn",
    )(q, k, v)


def _mla_kernel(q_ref, k_ref, vt_ref, o_ref):
    i = pl.program_id(1)
    tb = ATT_BLOCK
    key = lax.broadcasted_iota(jnp.int32, (tb, tb), 0)
    qry = lax.broadcasted_iota(jnp.int32, (tb, tb), 1)
    allowed = (key // CHUNK) <= (qry // CHUNK)
    scale = MLA_QK_DIM ** -0.5

    def scores(hd, kb):
        qh = q_ref[0, :, hd * HEAD_PAD:(hd + 1) * HEAD_PAD]
        start = pl.multiple_of(kb * tb, tb)
        ks = k_ref[0, pl.ds(start, tb), hd * HEAD_PAD:(hd + 1) * HEAD_PAD]
        vt = vt_ref[0, kb, hd * MLA_V_DIM:(hd + 1) * MLA_V_DIM, :]
        st = lax.dot_general(ks, qh, (((1,), (1,)), ((), ())),
                             preferred_element_type=jnp.float32) * (scale * LOG2E)
        return st, vt

    outs = []
    for grp in range(MLA_HEADS // MLA_HEAD_GROUP):
        heads = range(grp * MLA_HEAD_GROUP, (grp + 1) * MLA_HEAD_GROUP)

        def walk(kbs, st, masked):
            tiles = [[scores(hd, kb) for kb in kbs] for hd in heads]
            mids = []
            for per_head, (m, l, acc) in zip(tiles, st):
                ss = [jnp.where(allowed, s_t, -jnp.inf) if masked else s_t for s_t, _ in per_head]
                m_new = m
                for s_t in ss:
                    m_new = jnp.maximum(m_new, jnp.max(s_t, axis=0, keepdims=True))
                alpha = jnp.exp2(m - m_new)
                ps = [jnp.exp2(s_t - m_new) for s_t in ss]
                l = alpha * l
                for p in ps:
                    l = l + jnp.sum(p, axis=0, keepdims=True)
                mids.append((m_new, l, alpha, [p.astype(jnp.bfloat16) for p in ps],
                             [vt for _, vt in per_head], acc))
            out = []
            for m, l, alpha, ps, vts, acc in mids:
                acc = alpha * acc
                for vt, p in zip(vts, ps):
                    acc = acc + jnp.dot(vt, p, preferred_element_type=jnp.float32)
                out.append((m, l, acc))
            return out

        empty = (jnp.full((1, tb), -jnp.inf, jnp.float32), jnp.zeros((1, tb), jnp.float32),
                 jnp.zeros((MLA_V_DIM, tb), jnp.float32))
        state = walk([i], [empty] * len(heads), True)
        state = lax.cond(i % 2 == 1, lambda st: walk([i - 1], st, False), lambda st: st, state)
        state = lax.fori_loop(0, i // 2, lambda j, st: walk([2 * j, 2 * j + 1], st, False), state)
        outs += [acc / l for _, l, acc in state]
    o_ref[0] = jnp.concatenate(outs, axis=0).T


def _mla_call(q, k, vt):
    b, s, wq = q.shape
    _, nblk, wv, tb = vt.shape
    return pl.pallas_call(
        _mla_kernel,
        grid=(b, s // tb),
        in_specs=[pl.BlockSpec((1, tb, wq), lambda bi, i: (bi, i, 0)),
                  pl.BlockSpec((1, s, wq), lambda bi, i: (bi, 0, 0)),
                  pl.BlockSpec((1, nblk, wv, tb), lambda bi, i: (bi, 0, 0, 0))],
        out_specs=pl.BlockSpec((1, tb, wv), lambda bi, i: (bi, i, 0)),
        out_shape=jax.ShapeDtypeStruct((b, s, wv), jnp.float32),
        compiler_params=pltpu.CompilerParams(
            dimension_semantics=("parallel", "arbitrary"), vmem_limit_bytes=VMEM_LIMIT),
        name="mla_attn",
    )(q, k, vt)


def _topk_rows(a, k, ids=None):
    r, t = a.shape
    rows = lax.broadcasted_iota(jnp.int32, (r, t), 0) if ids is None else ids
    krow = lax.broadcasted_iota(jnp.int32, (k, t), 0)
    big = jnp.iinfo(jnp.int32).max

    def body(it, c):
        a, vals, idxs = c
        m = jnp.max(a, axis=0, keepdims=True)
        am = jnp.min(jnp.where(a == m, rows, big), axis=0, keepdims=True)
        a = jnp.where(rows == am, -jnp.inf, a)
        vals = jnp.where(krow == it, m, vals)
        idxs = jnp.where(krow == it, am, idxs)
        return a, vals, idxs

    _, vals, idxs = lax.fori_loop(
        0, k, body, (a, jnp.zeros((k, t), jnp.float32), jnp.zeros((k, t), jnp.int32)))
    return vals, idxs


def _select_rows(sel, table):
    out = jnp.zeros(sel.shape, table.dtype)
    for r in range(table.shape[0]):
        out = jnp.where(sel == r, table[r:r + 1, :], out)
    return out


def _candidate_tiles(k):
    tiles = [("row", 0, s) for s in range(0, k, 8)] + [("col", 0, s) for s in range(0, k, 8)]
    f = 1
    while (f + 1) * (f + 1) <= k:
        tiles += [("row", f, 0), ("col", f, 0)]
        f += 1
    seen, used, ids = set(), [], []
    for kind, fixed, start in tiles:
        tile_ids = []
        for r in range(8):
            pair = (fixed, start + r) if kind == "row" else (start + r, fixed)
            ok = (pair[0] + 1) * (pair[1] + 1) <= k and pair not in seen
            seen.add(pair)
            tile_ids.append(pair[0] * k + pair[1] if ok else -1)
        if max(tile_ids) >= 0:
            used.append((kind, fixed, start))
            ids += tile_ids
    want = sum(1 for a in range(k) for b in range(k) if (a + 1) * (b + 1) <= k)
    assert sum(i >= 0 for i in ids) == want, "candidate tiles must cover the whole staircase"
    return used, ids


_CAND_TILES, _CAND_IDS = _candidate_tiles(PEER_TOPK)


def _route_kernel(x_ref, sb_ref, mla_ref, sbn_ref, mlan_ref, wo_ref, fn_ref, wq_ref,
                  keys_ref, x2_ref, hn_ref, score_ref):
    bf = jnp.bfloat16
    sbn = _rms(sb_ref[...], sbn_ref[...]).astype(bf)
    mlan = _rms(mla_ref[...], mlan_ref[...]).astype(bf)
    attn = (jnp.dot(sbn, wo_ref[0:SB_WIDTH, :], preferred_element_type=jnp.float32)
            + jnp.dot(mlan, wo_ref[SB_WIDTH:SB_WIDTH + MLA_WIDTH, :],
                      preferred_element_type=jnp.float32))
    x2 = x_ref[...] + attn
    x2_ref[...] = x2
    hn = _rms(x2, fn_ref[...])
    hn_ref[...] = hn
    qp = jnp.dot(hn.astype(bf), wq_ref[...], preferred_element_type=jnp.float32)
    for hc in range(2 * PEER_HEADS):
        qhc = qp[:, hc * PEER_HALF_DIM:(hc + 1) * PEER_HALF_DIM].astype(bf)
        score_ref[0, hc] = lax.dot_general(keys_ref[hc], qhc, (((1,), (1,)), ((), ())),
                                           preferred_element_type=jnp.float32)


def _route_call(x2d, sb, mla, sbn, mlan, wo, fn, wq, keys):
    n = x2d.shape[0]
    t = EXPERT_TOKENS
    full = lambda a: pl.BlockSpec(a.shape, lambda i: (0,) * a.ndim)
    tok = lambda w: pl.BlockSpec((t, w), lambda i: (i, 0))
    return pl.pallas_call(
        _route_kernel,
        grid=(n // t,),
        in_specs=[tok(D_MODEL), tok(SB_WIDTH), tok(MLA_WIDTH), full(sbn), full(mlan),
                  full(wo), full(fn), full(wq), full(keys)],
        out_specs=[tok(D_MODEL), tok(D_MODEL),
                   pl.BlockSpec((1, 2 * PEER_HEADS, PEER_N_KEYS, t), lambda i: (i, 0, 0, 0))],
        out_shape=[jax.ShapeDtypeStruct((n, D_MODEL), jnp.float32),
                   jax.ShapeDtypeStruct((n, D_MODEL), jnp.float32),
                   jax.ShapeDtypeStruct((n // t, 2 * PEER_HEADS, PEER_N_KEYS, t), jnp.float32)],
        compiler_params=pltpu.CompilerParams(
            dimension_semantics=("parallel",), vmem_limit_bytes=VMEM_LIMIT),
        name="route",
    )(x2d, sb, mla, sbn, mlan, wo, fn, wq, keys)


def _expert_kernel(uv_hbm, hn_ref, x2_ref, sc0_ref, scn_ref, ids_ref, o_ref,
                   idx_vm, idx_smem, gate_vm, buf_even, buf_odd, idx_sem, row_sem):
    step = pl.program_id(0)
    nsteps = pl.num_programs(0)
    tb = EXPERT_TOKENS
    nu = GATHER_UNROLL
    ahead_b = GATHER_AHEAD
    per_buf = GATHER_SLOTS // 2
    bufs = (buf_even, buf_odd)
    head_bunches = tb // nu // PEER_HEADS
    assert ahead_b % 2 == 1 and GATHER_SLOTS == (ahead_b + 1) * nu
    assert (tb // nu) % (2 * (per_buf // nu)) == 0
    assert head_bunches % 2 == 0 and head_bunches >= 7
    cur = step % 2
    has_next = step + 1 < nsteps
    k = PEER_TOPK

    def candidates(s0, s1):
        pieces = []
        for kind, fixed, start in _CAND_TILES:
            if kind == "row":
                pieces.append(s0[fixed:fixed + 1, :] + s1[start:start + 8, :])
            else:
                pieces.append(s0[start:start + 8, :] + s1[fixed:fixed + 1, :])
        return jnp.where(ids_ref[...] >= 0, jnp.concatenate(pieces, axis=0), -jnp.inf)

    def finish(hd, slot, best, pos, i0, i1):
        e0 = _select_rows(pos // k, i0)
        e1 = _select_rows(pos % k, i1)
        p = jnp.exp(best - best[0:1, :])
        gate = p / jnp.sum(p, axis=0, keepdims=True)
        rows = pl.ds(hd * k if isinstance(hd, int) else pl.multiple_of(hd * k, k), k)
        idx_vm[rows, :] = e0 * PEER_N_KEYS + e1
        for tile in range(tb // 128):
            gate_vm[slot, tile, rows, :] = gate[:, tile * 128:(tile + 1) * 128]

    def topk_steps(state, its, ids=None):
        a, vals, idxs = state
        rows = lax.broadcasted_iota(jnp.int32, a.shape, 0) if ids is None else ids
        krow = lax.broadcasted_iota(jnp.int32, vals.shape, 0)
        big = jnp.iinfo(jnp.int32).max
        for it in its:
            m = jnp.max(a, axis=0, keepdims=True)
            am = jnp.min(jnp.where(a == m, rows, big), axis=0, keepdims=True)
            a = jnp.where(rows == am, -jnp.inf, a)
            vals = jnp.where(krow == it, m, vals)
            idxs = jnp.where(krow == it, am, idxs)
        return a, vals, idxs

    def fresh(a):
        return a, jnp.zeros((k, tb), jnp.float32), jnp.zeros((k, tb), jnp.int32)

    def idx_copy():
        return pltpu.make_async_copy(idx_vm, idx_smem, idx_sem)

    def place(b, j):
        par, half = b
        return par, (half % (per_buf // nu)) * nu + j

    def start_token(t, par, slot):
        base = slot * SLOT_ROWS
        for p in range(PEER_PICKS):
            src = pl.multiple_of(idx_smem[p, t] * SLAB_ROWS, SLAB_ROWS)
            pltpu.make_async_copy(
                uv_hbm.at[pl.ds(src, SLAB_ROWS), :],
                bufs[par].at[pl.ds(base + p * SLAB_PITCH, SLAB_ROWS), :],
                row_sem.at[par * per_buf + slot]).start(priority=p % 2)

    def wait_token(par, slot):
        rows = PEER_PICKS * SLAB_ROWS
        pltpu.make_async_copy(uv_hbm.at[pl.ds(0, rows), :],
                              bufs[par].at[pl.ds(slot * SLOT_ROWS, rows), :],
                              row_sem.at[par * per_buf + slot]).wait()

    @pl.when(step == 0)
    def _():
        def head(hd, _):
            s0, i0 = _topk_rows(sc0_ref[0, 2 * hd], k)
            s1, i1 = _topk_rows(sc0_ref[0, 2 * hd + 1], k)
            best, pos = _topk_rows(candidates(s0, s1), k, ids_ref[...])
            finish(hd, 0, best, pos, i0, i1)
            return 0

        lax.fori_loop(0, PEER_HEADS, head, 0)
        idx_copy().start()

    idx_copy().wait()
    for b in range(ahead_b):
        for j in range(nu):
            start_token(b * nu + j, *place((b % 2, b // 2), j))

    lane = lax.broadcasted_iota(jnp.int32, (PEER_PICKS, 128), 1)
    groups = PEER_PICKS // 8
    tiles = D_MODEL // 128

    def compute_token(t, par, slot):
        buf = bufs[par]
        base = slot * SLOT_ROWS
        xrow = hn_ref[pl.ds(t, 1), :]
        pres = []
        for g in range(groups):
            acc = None
            for c in range(tiles):
                u = buf[pl.ds(base + g * 8 * SLAB_PITCH + c, 8, stride=SLAB_PITCH), :]
                term = u * xrow[:, c * 128:(c + 1) * 128]
                acc = term if acc is None else acc + term
            pres.append(jnp.sum(acc, axis=1, keepdims=True))
        pre = jnp.concatenate(pres, axis=0)
        gate = jnp.sum(jnp.where(lane == t % 128, gate_vm[cur, t // 128], 0.0),
                       axis=1, keepdims=True)
        coef = gate * (0.5 * pre * (1.0 + lax.erf(pre * (2.0 ** -0.5))))
        outs = []
        for c in range(tiles):
            acc = None
            for g in range(groups):
                v = buf[pl.ds(base + g * 8 * SLAB_PITCH + tiles + c, 8, stride=SLAB_PITCH), :]
                term = coef[g * 8:(g + 1) * 8, :] * v
                acc = term if acc is None else acc + term
            outs.append(jnp.sum(acc, axis=0, keepdims=True))
        out = jnp.concatenate(outs, axis=1)
        o_ref[pl.ds(t, 1), :] = x2_ref[pl.ds(t, 1), :] + out

    def bunch(par, half, refill, search=None):
        first = (2 * half + par) * nu
        for j in range(nu):
            wait_token(*place((par, half), j))
        found = search() if search is not None else None
        for j in range(nu):
            compute_token(first + j, *place((par, half), j))

        if refill:
            later = (par + ahead_b) // 2 + half
            for j in range(nu):
                start_token(first + ahead_b * nu + j, *place((1 - par, later), j))
        return found

    half_its = (range(0, k // 2), range(k // 2, k))

    def head_body(hd, last):
        def run(j, search=None):
            return bunch(j % 2, hd * (head_bunches // 2) + j // 2,
                         not last or j < head_bunches - ahead_b, search)

        st0 = fresh(scn_ref[0, 2 * hd])
        st1 = fresh(scn_ref[0, 2 * hd + 1])
        st0 = run(0, lambda: topk_steps(st0, half_its[0]))
        st0 = run(1, lambda: topk_steps(st0, half_its[1]))
        st1 = run(2, lambda: topk_steps(st1, half_its[0]))
        st1 = run(3, lambda: topk_steps(st1, half_its[1]))
        ids = ids_ref[...]
        st2 = fresh(candidates(st0[1], st1[1]))
        st2 = run(4, lambda: topk_steps(st2, half_its[0], ids))
        st2 = run(5, lambda: topk_steps(st2, half_its[1], ids))
        run(6, lambda: finish(hd, 1 - cur, st2[1], st2[2], st0[2], st1[2]))
        for j in range(7, head_bunches):
            run(j)
        return 0

    lax.fori_loop(0, PEER_HEADS - 1, lambda hd, c: head_body(hd, False), 0)
    head_body(PEER_HEADS - 1, True)

    @pl.when(has_next)
    def _():
        idx_copy().start()


def _expert_call(uv, hn, x2, scores):
    n = hn.shape[0]
    tb = EXPERT_TOKENS
    nblk = n // tb
    tok = pl.BlockSpec((tb, D_MODEL), lambda i: (i, 0))
    sc_shape = (1,) + scores.shape[1:]
    ids = jnp.broadcast_to(jnp.asarray(_CAND_IDS, jnp.int32)[:, None], (len(_CAND_IDS), tb))
    return pl.pallas_call(
        _expert_kernel,
        grid=(nblk,),
        in_specs=[pl.BlockSpec(memory_space=pl.ANY),
                  tok, tok,
                  pl.BlockSpec(sc_shape, lambda i: (0, 0, 0, 0)),
                  pl.BlockSpec(sc_shape, lambda i: (jnp.minimum(i + 1, nblk - 1), 0, 0, 0)),
                  pl.BlockSpec(ids.shape, lambda i: (0, 0))],
        out_specs=tok,
        out_shape=jax.ShapeDtypeStruct((n, D_MODEL), jnp.float32),
        scratch_shapes=[pltpu.VMEM((PEER_PICKS, tb), jnp.int32),
                        pltpu.SMEM((PEER_PICKS, tb), jnp.int32),
                        pltpu.VMEM((2, tb // 128, PEER_PICKS, 128), jnp.float32),
                        pltpu.VMEM((GATHER_SLOTS // 2 * SLOT_ROWS, 128), jnp.float32),
                        pltpu.VMEM((GATHER_SLOTS // 2 * SLOT_ROWS, 128), jnp.float32),
                        pltpu.SemaphoreType.DMA,
                        pltpu.SemaphoreType.DMA((GATHER_SLOTS,))],
        compiler_params=pltpu.CompilerParams(
            dimension_semantics=("arbitrary",), vmem_limit_bytes=VMEM_LIMIT),
        name="experts",
    )(uv, hn, x2, scores, scores, ids)


def _pad_heads(w, head_dim):
    k = w.shape[0]
    w = w.reshape(k, -1, head_dim)
    w = jnp.pad(w, ((0, 0), (0, 0), (0, HEAD_PAD - head_dim)))
    return w.reshape(k, -1)


def _layer(x2d, pos, b, s, an, w_in, cqn, w_uq, ckvn, w_ukv, qn, kn, sbn, mlan, w_o, fn,
           w_pq, sub_keys, pu, pv):
    bf = jnp.bfloat16
    row = lambda a: a.reshape(1, -1)
    kpe_cols = jnp.pad(w_in[:, _C_KPE:_C_KPE + MLA_ROPE_DIM],
                       ((0, 0), (MLA_NOPE_DIM, HEAD_PAD - MLA_QK_DIM)))
    win = jnp.concatenate([w_in[:, :_C_KPE], kpe_cols], axis=1).astype(bf)
    assert win.shape == (D_MODEL, IN_PAD_WIDTH)
    wuq = _pad_heads(w_uq, MLA_QK_DIM).astype(bf)
    w_ukv3 = w_ukv.reshape(MLA_KV_RANK, MLA_HEADS, MLA_NOPE_DIM + MLA_V_DIM)
    wuk = _pad_heads(w_ukv3[:, :, :MLA_NOPE_DIM].reshape(MLA_KV_RANK, -1), MLA_NOPE_DIM).astype(bf)
    wuv = w_ukv3[:, :, MLA_NOPE_DIM:].reshape(MLA_KV_RANK, -1).astype(bf)
    pad_gain = lambda g: jnp.pad(g, (0, HEAD_PAD - MLA_QK_DIM)).reshape(1, HEAD_PAD)
    half = MLA_ROPE_DIM // 2
    inv_freq = 1.0 / (ROPE_THETA ** (jnp.arange(half, dtype=jnp.float32) * (2.0 / MLA_ROPE_DIM)))
    freq = jnp.concatenate([jnp.zeros((MLA_NOPE_DIM,), jnp.float32), inv_freq, inv_freq,
                            jnp.zeros((HEAD_PAD - MLA_QK_DIM,), jnp.float32)]).reshape(1, HEAD_PAD)

    sbq, sbk, sbv, q, k, v = _prep_call(
        x2d, pos, row(an), win, row(cqn), wuq, row(ckvn), wuk, wuv,
        pad_gain(qn), pad_gain(kn), freq)

    r3 = lambda a: a.reshape(b, s, a.shape[-1])
    sb = _sb_call(r3(sbq), r3(sbk), r3(sbv)).reshape(b * s, SB_WIDTH)
    vt = v.reshape(b, s // ATT_BLOCK, MLA_WIDTH, ATT_BLOCK)
    mla = _mla_call(r3(q), r3(k), vt).reshape(b * s, MLA_WIDTH)

    keys = sub_keys.transpose(1, 0, 2, 3).reshape(2 * PEER_HEADS, PEER_N_KEYS, PEER_HALF_DIM).astype(bf)
    x2, hn, scores = _route_call(
        x2d, sb, mla, row(sbn), row(mlan), w_o.astype(bf), row(fn), w_pq.astype(bf), keys)

    slab = lambda w: w.reshape(PEER_N_EXPERTS, D_MODEL // 128, 128)
    uv = jnp.concatenate([slab(pu), slab(pv)], axis=1).reshape(PEER_N_EXPERTS * SLAB_ROWS, 128)
    return _expert_call(uv, hn, x2, scores)


def kernel(x, positions, attn_norm, w_in, cq_norm, w_uq, ckv_norm, w_ukv, q_norm, k_norm,
           sb_out_norm, mla_out_norm, w_o, ffn_norm, peer_w_q, peer_sub_keys, peer_u, peer_v):
    b, s, d = x.shape
    x2d = x.reshape(b * s, d)
    pos = positions.reshape(b * s, 1)
    for l in range(attn_norm.shape[0]):
        x2d = _layer(x2d, pos, b, s, attn_norm[l], w_in[l], cq_norm[l], w_uq[l], ckv_norm[l],
                     w_ukv[l], q_norm[l], k_norm[l], sb_out_norm[l], mla_out_norm[l], w_o[l],
                     ffn_norm[l], peer_w_q[l], peer_sub_keys[l], peer_u[l], peer_v[l])
    return x2d.reshape(b, s, d)
```
